```python
import jax, jax.numpy as jnp
from jax import lax
import numpy as np

D_MODEL = 1024
BATCH = 4
SEQ = 4096
DEPTH = 2

CHUNK = 128
A_GROUPS = 8
A_WIDTH = 1024
A_GROUP_DIM = A_WIDTH // A_GROUPS
POOL_WINDOWS = (2, 4, 8, 16)
B_GROUPS = len(POOL_WINDOWS)
B_GROUP_DIM = 128
B_WIDTH = B_GROUPS * B_GROUP_DIM
C_WIDTH = 512
CONV_WIDTH = 3
D_HEADS = 8
HEAD_DIM = 64
D_WIDTH = D_HEADS * HEAD_DIM
ROT_DIM = HEAD_DIM // 4
ROPE_THETA = 500000.0
MOBA_BLOCK = 256
MOBA_TOPK = 3
MOBA_Q_CHUNK = 64
D_FF = 2816
N_EVEN = (DEPTH + 1) // 2
N_ODD = DEPTH // 2
EPS = 1e-6

kernel_name = "hybrid_gmlp_pool_shortconv_moba_block"


def rms_norm(x, g):
    xf = x.astype(jnp.float32)
    y = xf * lax.rsqrt(jnp.mean(xf * xf, axis=-1, keepdims=True) + EPS)
    return (y * g.astype(jnp.float32)).astype(x.dtype)


def layer_norm(x, g, b):
    xf = x.astype(jnp.float32)
    mu = jnp.mean(xf, axis=-1, keepdims=True)
    var = jnp.mean(jnp.square(xf - mu), axis=-1, keepdims=True)
    y = (xf - mu) * lax.rsqrt(var + EPS)
    return (y * g.astype(jnp.float32) + b.astype(jnp.float32)).astype(x.dtype)


def causal_dwconv(x, w):
    k = w.shape[0]
    return lax.conv_general_dilated(
        x, w[:, None, :].astype(x.dtype), window_strides=(1,), padding=[(k - 1, 0)],
        dimension_numbers=('NWC', 'WIO', 'NWC'), feature_group_count=x.shape[-1])


def spatial_gating(u, v, v_g, v_b, w_s, b_s):
    bsz, s, _ = v.shape
    v = layer_norm(v, v_g, v_b)
    vc = v.reshape(bsz, s // CHUNK, CHUNK, A_GROUPS, A_GROUP_DIM)
    causal = jnp.tril(jnp.ones((CHUNK, CHUNK), dtype=bool))
    w = jnp.where(causal, w_s, jnp.zeros((), w_s.dtype))
    mixed = jnp.einsum('gij,bnjgd->bnigd', w, vc) + b_s.T[None, None, :, :, None]
    return u * mixed.reshape(bsz, s, A_WIDTH)


def multiscale_pool(xb, pool_w, pool_scale):
    bsz, s, _ = xb.shape
    xf = xb.astype(jnp.float32).reshape(bsz, s, B_GROUPS, B_GROUP_DIM)
    cs = jnp.cumsum(xf, axis=1)
    t = jnp.arange(s)
    outs = []
    for gi, w in enumerate(POOL_WINDOWS):
        csg = cs[:, :, gi]
        lag = jnp.pad(csg[:, :s - w], ((0, 0), (w, 0), (0, 0)))
        cnt = jnp.minimum(t + 1, w).astype(jnp.float32)[None, :, None]
        outs.append((csg - lag) / cnt - xf[:, :, gi])
    pooled = jnp.stack(outs, axis=2).astype(xb.dtype)
    mixed = jnp.einsum('bsgd,gde->bsge', pooled, pool_w).reshape(bsz, s, B_WIDTH)
    return mixed * pool_scale


def partial_rotary(x, pos):
    half = ROT_DIM // 2
    inv_freq = ROPE_THETA ** (-jnp.arange(0, ROT_DIM, 2, dtype=jnp.float32) / ROT_DIM)
    ang = pos.astype(jnp.float32)[:, None, :, None] * inv_freq
    cos, sin = jnp.cos(ang), jnp.sin(ang)
    xr = x[..., :ROT_DIM].astype(jnp.float32)
    x1, x2 = xr[..., :half], xr[..., half:]
    rot = jnp.concatenate([x1 * cos - x2 * sin, x2 * cos + x1 * sin], axis=-1)
    return jnp.concatenate([rot.astype(x.dtype), x[..., ROT_DIM:]], axis=-1)


def moba_attention(q, k, v):
    bsz, nh, s, dh = q.shape
    nb = -(-s // MOBA_BLOCK)
    pad = nb * MOBA_BLOCK - s
    kp = jnp.pad(k, ((0, 0), (0, 0), (0, pad), (0, 0)))
    vp = jnp.pad(v, ((0, 0), (0, 0), (0, pad), (0, 0)))
    k_blocks = kp.reshape(bsz, nh, nb, MOBA_BLOCK, dh)
    v_blocks = vp.reshape(bsz, nh, nb, MOBA_BLOCK, dh)
    k_mean = jnp.mean(k_blocks.astype(jnp.float32), axis=3)
    topk = min(MOBA_TOPK, nb)
    scale = dh ** -0.5
    bi = jnp.arange(bsz)[:, None, None, None]
    hi = jnp.arange(nh)[None, :, None, None]

    def chunk(start):
        cur = start // MOBA_BLOCK
        qf = lax.dynamic_slice_in_dim(q, start, MOBA_Q_CHUNK, axis=2).astype(jnp.float32)
        gate = jnp.einsum('bhqd,bhnd->bhqn', qf, k_mean)
        gate = jnp.where(jnp.arange(nb) < cur, gate, -jnp.inf)
        _, idx = lax.top_k(gate, topk)
        valid = jnp.arange(topk) < cur
        k_sel = k_blocks[bi, hi, idx].astype(jnp.float32)
        v_sel = v_blocks[bi, hi, idx].astype(jnp.float32)
        s_sel = jnp.einsum('bhqd,bhqnld->bhqnl', qf, k_sel) * scale
        s_sel = jnp.where(valid[:, None], s_sel, -jnp.inf)
        k_own = lax.dynamic_slice_in_dim(kp, cur * MOBA_BLOCK, MOBA_BLOCK, axis=2).astype(jnp.float32)
        v_own = lax.dynamic_slice_in_dim(vp, cur * MOBA_BLOCK, MOBA_BLOCK, axis=2).astype(jnp.float32)
        s_own = jnp.einsum('bhqd,bhld->bhql', qf, k_own) * scale
        q_pos = start + jnp.arange(MOBA_Q_CHUNK)
        k_pos = cur * MOBA_BLOCK + jnp.arange(MOBA_BLOCK)
        s_own = jnp.where(k_pos[None, :] <= q_pos[:, None], s_own, -jnp.inf)
        logits = jnp.concatenate(
            [s_sel.reshape(bsz, nh, MOBA_Q_CHUNK, topk * MOBA_BLOCK), s_own], axis=-1)
        p = jax.nn.softmax(logits, axis=-1)
        p_sel = p[..., :topk * MOBA_BLOCK].reshape(bsz, nh, MOBA_Q_CHUNK, topk, MOBA_BLOCK)
        p_own = p[..., topk * MOBA_BLOCK:]
        o = (jnp.einsum('bhqnl,bhqnld->bhqd', p_sel, v_sel)
             + jnp.einsum('bhql,bhld->bhqd', p_own, v_own))
        return o.astype(q.dtype)

    starts = jnp.arange(s // MOBA_Q_CHUNK) * MOBA_Q_CHUNK
    out = lax.map(chunk, starts)
    return out.transpose(1, 2, 0, 3, 4).reshape(bsz, nh, s, dh)


def mixer_ab(h, w_in, v_g, v_b, w_s, b_s, pool_w, pool_scale, w_out):
    proj = h @ w_in
    u, v, xb = jnp.split(proj, [A_WIDTH, 2 * A_WIDTH], axis=-1)
    ya = spatial_gating(jax.nn.gelu(u), jax.nn.gelu(v), v_g, v_b, w_s, b_s)
    yb = multiscale_pool(xb, pool_w, pool_scale)
    return jnp.concatenate([ya, yb], axis=-1) @ w_out


def mixer_cd(h, pos, w_in, conv_w, w_out):
    bsz, s, _ = h.shape
    proj = h @ w_in
    cb, cc, cx, q, k, v = jnp.split(
        proj, [C_WIDTH, 2 * C_WIDTH, 3 * C_WIDTH, 3 * C_WIDTH + D_WIDTH, 3 * C_WIDTH + 2 * D_WIDTH], axis=-1)
    yc = cb * causal_dwconv(cc * cx, conv_w)

    def heads(t):
        return t.reshape(bsz, s, D_HEADS, HEAD_DIM).transpose(0, 2, 1, 3)

    qh = partial_rotary(heads(q), pos)
    kh = partial_rotary(heads(k), pos)
    yd = moba_attention(qh, kh, heads(v)).transpose(0, 2, 1, 3).reshape(bsz, s, D_WIDTH)
    return jnp.concatenate([yc, yd], axis=-1) @ w_out


def conv_ffn(h, w_up, conv_w, w_down):
    up = causal_dwconv(h @ w_up, conv_w)
    g, u = jnp.split(up, 2, axis=-1)
    return (jax.nn.silu(g) * u) @ w_down


def setup_inputs(seed: int = 0) -> dict:
    key = jax.random.key(seed)
    ks = jax.random.split(key, 24)
    f32 = jnp.float32

    def dense(k, shape, fan_in):
        return jax.random.normal(k, shape, f32) * (fan_in ** -0.5)

    def gain(k, shape):
        return 1.0 + 0.1 * jax.random.normal(k, shape, f32)

    x = jax.random.normal(ks[0], (BATCH, SEQ, D_MODEL), f32)
    c = jax.random.normal(ks[1], (BATCH, D_MODEL), f32)
    offset = jax.random.randint(ks[2], (BATCH,), 0, 2048, dtype=jnp.int32)
    positions = offset[:, None] + jnp.arange(SEQ, dtype=jnp.int32)[None, :]
    return {
        'x': x,
        'c': c,
        'positions': positions,
        'ab_w_in': dense(ks[3], (N_EVEN, D_MODEL, 2 * A_WIDTH + B_WIDTH), D_MODEL),
        'ab_vnorm_g': gain(ks[4], (N_EVEN, A_WIDTH)),
        'ab_vnorm_b': 0.1 * jax.random.normal(ks[5], (N_EVEN, A_WIDTH), f32),
        'ab_spatial_w': dense(ks[6], (N_EVEN, A_GROUPS, CHUNK, CHUNK), CHUNK),
        'ab_spatial_b': gain(ks[7], (N_EVEN, A_GROUPS, CHUNK)),
        'ab_pool_w': dense(ks[8], (N_EVEN, B_GROUPS, B_GROUP_DIM, B_GROUP_DIM), B_GROUP_DIM),
        'ab_pool_scale': gain(ks[9], (N_EVEN, B_WIDTH)),
        'ab_w_out': dense(ks[10], (N_EVEN, A_WIDTH + B_WIDTH, D_MODEL), A_WIDTH + B_WIDTH),
        'cd_w_in': dense(ks[11], (N_ODD, D_MODEL, 3 * C_WIDTH + 3 * D_WIDTH), D_MODEL),
        'cd_conv_w': dense(ks[12], (N_ODD, CONV_WIDTH, C_WIDTH), CONV_WIDTH),
        'cd_w_out': dense(ks[13], (N_ODD, C_WIDTH + D_WIDTH, D_MODEL), C_WIDTH + D_WIDTH),
        'ffn_w_up': dense(ks[14], (DEPTH, D_MODEL, 2 * D_FF), D_MODEL),
        'ffn_conv_w': dense(ks[15], (DEPTH, CONV_WIDTH, 2 * D_FF), CONV_WIDTH),
        'ffn_w_down': dense(ks[16], (DEPTH, D_FF, D_MODEL), D_FF),
        'ada_w': 0.02 * jax.random.normal(ks[17], (DEPTH, D_MODEL, 6 * D_MODEL), f32),
        'ada_b': 0.02 * jax.random.normal(ks[18], (DEPTH, 6 * D_MODEL), f32),
        'norm_g': gain(ks[19], (DEPTH, 4, D_MODEL)),
    }


def reference(x, c, positions, ab_w_in, ab_vnorm_g, ab_vnorm_b, ab_spatial_w, ab_spatial_b,
              ab_pool_w, ab_pool_scale, ab_w_out, cd_w_in, cd_conv_w, cd_w_out,
              ffn_w_up, ffn_conv_w, ffn_w_down, ada_w, ada_b, norm_g):
    c_act = jax.nn.silu(c)
    for i in range(DEPTH):
        mod = (c_act @ ada_w[i] + ada_b[i])[:, None, :]
        sh1, sc1, gt1, sh2, sc2, gt2 = jnp.split(mod, 6, axis=-1)
        g = norm_g[i]
        h = rms_norm(x, g[0]) * (1.0 + sc1) + sh1
        j = i // 2
        if i % 2 == 0:
            y = mixer_ab(h, ab_w_in[j], ab_vnorm_g[j], ab_vnorm_b[j], ab_spatial_w[j],
                         ab_spatial_b[j], ab_pool_w[j], ab_pool_scale[j], ab_w_out[j])
        else:
            y = mixer_cd(h, positions, cd_w_in[j], cd_conv_w[j], cd_w_out[j])
        x = x + gt1 * rms_norm(y, g[1])
        h = rms_norm(x, g[2]) * (1.0 + sc2) + sh2
        y = conv_ffn(h, ffn_w_up[i], ffn_conv_w[i], ffn_w_down[i])
        x = x + gt2 * rms_norm(y, g[3])
    return x
```

```python
import functools

import jax
import jax.numpy as jnp
from jax import lax
from jax.experimental import pallas as pl
from jax.experimental.pallas import tpu as pltpu

F32 = jnp.float32
BF16 = jnp.bfloat16

D_MODEL = 1024
CHUNK = 128
A_GROUPS = 8
A_WIDTH = 1024
POOL_WINDOWS = (2, 4, 8, 16)
B_GROUP_DIM = 128
B_WIDTH = len(POOL_WINDOWS) * B_GROUP_DIM
POOL_HALO = 16
C_WIDTH = 512
CONV_WIDTH = 3
CONV_HALO = 8
D_HEADS = 8
HEAD_DIM = 64
D_WIDTH = D_HEADS * HEAD_DIM
ROT_DIM = HEAD_DIM // 4
ROPE_THETA = 500000.0
MOBA_BLOCK = 256
MOBA_TOPK = 3
D_FF = 2816
EPS = 1e-6

LANES = 128
TOKEN_TILE = 256
FF_CHUNK = 256
VMEM_LIMIT = 56 * 1024 * 1024

SH1, SC1, GT1, SH2, SC2, GT2 = range(6)


def _dot(a, b):
    return jnp.dot(a, b, preferred_element_type=F32)


def _rms(x):
    return x * lax.rsqrt(jnp.mean(x * x, axis=-1, keepdims=True) + EPS)


def _gelu_tanh(x):
    inner = 0.7978845608028654 * (x + 0.044715 * (x * x * x))
    return x * (0.5 * (1.0 + jnp.tanh(inner)))


def _sigmoid(x):
    return 1.0 / (1.0 + jnp.exp(-x))


def _modulated_norm(x, gain, shift, scale):
    return _rms(x) * (gain * (1.0 + scale)) + shift


def _causal_conv3(work_ref, halo_ref, cols, cur, w):
    rows = cur.shape[0]
    work_ref[0:CONV_HALO, :] = halo_ref[:, cols]
    work_ref[CONV_HALO:CONV_HALO + rows, :] = cur
    out = (w[0:1] * work_ref[CONV_HALO - 2:CONV_HALO - 2 + rows, :]
           + w[1:2] * work_ref[CONV_HALO - 1:CONV_HALO - 1 + rows, :]
           + w[2:3] * cur)
    halo_ref[:, cols] = work_ref[rows:rows + CONV_HALO, :]
    return out


def _mod_kernel(c_ref, w_ref, b_ref, o_ref):
    c = c_ref[...]
    act = c * _sigmoid(c)
    o_ref[...] = jnp.dot(act, w_ref[...], preferred_element_type=F32,
                         precision=lax.Precision.HIGHEST) + b_ref[...]


def _ada_mod(c, ada_w, ada_b):
    depth, d, n = ada_w.shape
    bsz = c.shape[0]
    rows = 8
    c_pad = jnp.pad(c, ((0, rows - bsz), (0, 0)))
    tn = 1536
    out = pl.pallas_call(
        _mod_kernel,
        grid=(depth, n // tn),
        in_specs=[
            pl.BlockSpec((rows, d), lambda l, j: (0, 0)),
            pl.BlockSpec((None, d, tn), lambda l, j: (l, 0, j)),
            pl.BlockSpec((None, 1, tn), lambda l, j: (l, 0, j)),
        ],
        out_specs=pl.BlockSpec((None, rows, tn), lambda l, j: (l, 0, j)),
        out_shape=jax.ShapeDtypeStruct((depth, rows, n), F32),
        compiler_params=pltpu.CompilerParams(
            dimension_semantics=("arbitrary", "arbitrary"),
            vmem_limit_bytes=VMEM_LIMIT),
        name="ada_mod",
    )(c_pad, ada_w, ada_b.reshape(depth, 1, n))
    return out[:, :bsz].reshape(depth, bsz, 6, d)


def _mixer_ab_kernel(x_ref, mod_ref, g_ref, win_ref, vg_ref, vb_ref, ws_ref, bsf_ref,
                     pw_ref, ps_ref, wout_ref, o_ref, xb_ref, ycat_ref):
    t = pl.program_id(1)
    tm = x_ref.shape[0]

    @pl.when(t == 0)
    def _():
        xb_ref[0:POOL_HALO, :] = jnp.zeros((POOL_HALO, B_WIDTH), F32)

    x = x_ref[...]
    mod = mod_ref[...]
    g = g_ref[...]
    h = _modulated_norm(x, g[0:1], mod[SH1:SH1 + 1], mod[SC1:SC1 + 1]).astype(BF16)

    gv = _gelu_tanh(_dot(h, win_ref[:, A_WIDTH:2 * A_WIDTH]))
    mu = jnp.mean(gv, axis=-1, keepdims=True)
    dv = gv - mu
    var = jnp.mean(dv * dv, axis=-1, keepdims=True)
    vn = (dv * lax.rsqrt(var + EPS) * vg_ref[...] + vb_ref[...]).astype(BF16)
    gu = _gelu_tanh(_dot(h, win_ref[:, 0:A_WIDTH]))

    row = lax.broadcasted_iota(jnp.int32, (CHUNK, CHUNK), 0)
    col = lax.broadcasted_iota(jnp.int32, (CHUNK, CHUNK), 1)
    tril = row >= col
    for gi in range(A_GROUPS):
        cols = slice(gi * LANES, (gi + 1) * LANES)
        wg = jnp.where(tril, ws_ref[gi], 0.0).astype(BF16)
        bias = bsf_ref[:, cols]
        for ci in range(tm // CHUNK):
            rows = slice(ci * CHUNK, (ci + 1) * CHUNK)
            mixed = _dot(wg, vn[rows, cols]) + bias
            ycat_ref[rows, cols] = (gu[rows, cols] * mixed).astype(BF16)

    xb = _dot(h, win_ref[:, 2 * A_WIDTH:2 * A_WIDTH + B_WIDTH])
    xb_ref[POOL_HALO:POOL_HALO + tm, :] = xb
    pos = t * tm + lax.broadcasted_iota(jnp.int32, (tm, 1), 0)
    for gi, w in enumerate(POOL_WINDOWS):
        cols = slice(gi * LANES, (gi + 1) * LANES)
        s = xb[:, cols]
        for k in range(1, w):
            s = s + xb_ref[POOL_HALO - k:POOL_HALO - k + tm, cols]
        cnt = jnp.minimum(pos + 1, w).astype(F32)
        pooled = s * (1.0 / cnt) - xb[:, cols]
        mixed = _dot(pooled.astype(BF16), pw_ref[gi]) * ps_ref[:, cols]
        ycat_ref[:, A_WIDTH + gi * LANES:A_WIDTH + (gi + 1) * LANES] = mixed.astype(BF16)
    xb_ref[0:POOL_HALO, :] = xb_ref[tm:tm + POOL_HALO, :]

    y = _dot(ycat_ref[...], wout_ref[...])
    o_ref[...] = x + mod[GT1:GT1 + 1] * (_rms(y) * g[1:2])


def _const_spec(shape):
    zeros = (0,) * len(shape)
    return pl.BlockSpec(shape, lambda b, t: zeros, pipeline_mode=pl.Buffered(1))


def _mixer_ab(x, mod, g, w_in, vnorm_g, vnorm_b, w_s, b_s, pool_w, pool_scale, w_out):
    bsz, s, d = x.shape
    tm = TOKEN_TILE
    bias_full = jnp.repeat(b_s.T, LANES, axis=1)
    return pl.pallas_call(
        _mixer_ab_kernel,
        grid=(bsz, s // tm),
        in_specs=[
            pl.BlockSpec((None, tm, d), lambda b, t: (b, t, 0)),
            pl.BlockSpec((None, 6, d), lambda b, t: (b, 0, 0)),
            _const_spec((4, d)),
            _const_spec(w_in.shape),
            _const_spec((1, A_WIDTH)),
            _const_spec((1, A_WIDTH)),
            _const_spec(w_s.shape),
            _const_spec(bias_full.shape),
            _const_spec(pool_w.shape),
            _const_spec((1, B_WIDTH)),
            _const_spec(w_out.shape),
        ],
        out_specs=pl.BlockSpec((None, tm, d), lambda b, t: (b, t, 0)),
        out_shape=jax.ShapeDtypeStruct(x.shape, F32),
        scratch_shapes=[
            pltpu.VMEM((POOL_HALO + tm, B_WIDTH), F32),
            pltpu.VMEM((tm, A_WIDTH + B_WIDTH), BF16),
        ],
        compiler_params=pltpu.CompilerParams(
            dimension_semantics=("arbitrary", "arbitrary"),
            vmem_limit_bytes=VMEM_LIMIT),
        name="mixer_ab",
    )(x, mod, g, w_in.astype(BF16), vnorm_g.reshape(1, -1), vnorm_b.reshape(1, -1),
      w_s, bias_full, pool_w.astype(BF16), pool_scale.reshape(1, -1), w_out.astype(BF16))


def _ffn_kernel(fuse_mixer_out, *refs):
    if fuse_mixer_out:
        (x_ref, yc_ref, yd_ref, wo_ref, mod_ref, g_ref, wup_ref, cw_ref, wdn_ref,
         o_ref, halo_ref, work_ref, act_ref) = refs
    else:
        (x_ref, mod_ref, g_ref, wup_ref, cw_ref, wdn_ref,
         o_ref, halo_ref, work_ref, act_ref) = refs
    t = pl.program_id(1)

    @pl.when(t == 0)
    def _():
        halo_ref[...] = jnp.zeros(halo_ref.shape, F32)

    x = x_ref[...]
    mod = mod_ref[...]
    g = g_ref[...]
    if fuse_mixer_out:
        y = _dot(yc_ref[...], wo_ref[0:C_WIDTH, :]) + _dot(yd_ref[...], wo_ref[C_WIDTH:, :])
        x = x + mod[GT1:GT1 + 1] * (_rms(y) * g[1:2])
    h = _modulated_norm(x, g[2:3], mod[SH2:SH2 + 1], mod[SC2:SC2 + 1]).astype(BF16)

    for j in range(D_FF // FF_CHUNK):
        halves = []
        for half in range(2):
            start = half * D_FF + j * FF_CHUNK
            cols = slice(start, start + FF_CHUNK)
            up = _dot(h, wup_ref[:, cols])
            halves.append(_causal_conv3(work_ref.at[half], halo_ref, cols, up, cw_ref[:, cols]))
        gate, lin = halves
        act = gate * _sigmoid(gate) * lin
        act_ref[:, j * FF_CHUNK:(j + 1) * FF_CHUNK] = act.astype(BF16)

    y = _dot(act_ref[...], wdn_ref[...])
    o_ref[...] = x + mod[GT2:GT2 + 1] * (_rms(y) * g[3:4])


def _ffn(x, mod, g, w_up, conv_w, w_down, mixer_out=None):
    bsz, s, d = x.shape
    tm = TOKEN_TILE
    tile = lambda width: pl.BlockSpec((None, tm, width), lambda b, t: (b, t, 0))
    in_specs = [tile(d)]
    args = [x]
    if mixer_out is not None:
        yc, yd, w_o = mixer_out
        in_specs += [tile(C_WIDTH), tile(D_WIDTH), _const_spec(w_o.shape)]
        args += [yc, yd, w_o.astype(BF16)]
    in_specs += [
        pl.BlockSpec((None, 6, d), lambda b, t: (b, 0, 0)),
        _const_spec((4, d)),
        _const_spec(w_up.shape),
        _const_spec(conv_w.shape),
        _const_spec(w_down.shape),
    ]
    args += [mod, g, w_up.astype(BF16), conv_w, w_down.astype(BF16)]
    return pl.pallas_call(
        functools.partial(_ffn_kernel, mixer_out is not None),
        grid=(bsz, s // tm),
        in_specs=in_specs,
        out_specs=tile(d),
        out_shape=jax.ShapeDtypeStruct(x.shape, F32),
        scratch_shapes=[
            pltpu.VMEM((CONV_HALO, 2 * D_FF), F32),
            pltpu.VMEM((2, CONV_HALO + tm, FF_CHUNK), F32),
            pltpu.VMEM((tm, D_FF), BF16),
        ],
        compiler_params=pltpu.CompilerParams(
            dimension_semantics=("arbitrary", "arbitrary"),
            vmem_limit_bytes=VMEM_LIMIT),
        name="conv_ffn_fused" if mixer_out is not None else "conv_ffn",
    )(*args)


def _mixer_cd_in_kernel(x_ref, mod_ref, g_ref, pos_ref, invf_ref, win_ref, cw_ref,
                        yc_ref, qt_ref, k_ref, vt_ref, halo_ref, work_ref):
    t = pl.program_id(1)
    tm = x_ref.shape[0]

    @pl.when(t == 0)
    def _():
        halo_ref[...] = jnp.zeros(halo_ref.shape, F32)

    mod = mod_ref[...]
    g = g_ref[...]
    h = _modulated_norm(x_ref[...], g[0:1], mod[SH1:SH1 + 1], mod[SC1:SC1 + 1]).astype(BF16)

    def proj(i):
        return _dot(h, win_ref[:, i * C_WIDTH:(i + 1) * C_WIDTH])

    prod = proj(1) * proj(2)
    conv = _causal_conv3(work_ref, halo_ref, slice(0, C_WIDTH), prod, cw_ref[...])
    yc_ref[...] = (proj(0) * conv).astype(BF16)

    ang = pos_ref[...].astype(F32) * invf_ref[...]
    lane = lax.broadcasted_iota(jnp.int32, (1, LANES), 1) % HEAD_DIM
    half = ROT_DIM // 2
    cos = jnp.where(lane < ROT_DIM, jnp.cos(ang), 1.0)
    sin = jnp.sin(ang)
    sin = jnp.where(lane < half, -sin, jnp.where(lane < ROT_DIM, sin, 0.0))

    def rope(v):
        outs = []
        for ci in range(v.shape[1] // LANES):
            vc = v[:, ci * LANES:(ci + 1) * LANES]
            partner = jnp.where(lane < half, pltpu.roll(vc, LANES - half, 1),
                                pltpu.roll(vc, half, 1))
            outs.append(vc * cos + partner * sin)
        return jnp.concatenate(outs, axis=1)

    q = rope(proj(3))
    k = rope(proj(4))
    v = proj(5)
    for bi in range(tm // MOBA_BLOCK):
        rows = slice(bi * MOBA_BLOCK, (bi + 1) * MOBA_BLOCK)
        qt_ref[bi] = q[rows].T.astype(BF16)
        vt_ref[bi] = v[rows].T.astype(BF16)
    k_ref[...] = k.astype(BF16)


def _mixer_cd_in(x, mod, g, positions, w_in, conv_w):
    bsz, s, d = x.shape
    tm = TOKEN_TILE
    nb = s // MOBA_BLOCK
    bpt = tm // MOBA_BLOCK
    lane = jnp.arange(LANES) % HEAD_DIM
    inv_freq = ROPE_THETA ** (-jnp.arange(0, ROT_DIM, 2, dtype=F32) / ROT_DIM)
    invf = jnp.where(lane < ROT_DIM, inv_freq[lane % (ROT_DIM // 2)], 0.0).reshape(1, LANES)
    t_spec = pl.BlockSpec((None, bpt, D_WIDTH, MOBA_BLOCK), lambda b, t: (b, t, 0, 0))
    yc, qt, k, vt = pl.pallas_call(
        _mixer_cd_in_kernel,
        grid=(bsz, s // tm),
        in_specs=[
            pl.BlockSpec((None, tm, d), lambda b, t: (b, t, 0)),
            pl.BlockSpec((None, 6, d), lambda b, t: (b, 0, 0)),
            _const_spec((4, d)),
            pl.BlockSpec((None, tm, 1), lambda b, t: (b, t, 0)),
            _const_spec((1, LANES)),
            _const_spec(w_in.shape),
            _const_spec(conv_w.shape),
        ],
        out_specs=[
            pl.BlockSpec((None, tm, C_WIDTH), lambda b, t: (b, t, 0)),
            t_spec,
            pl.BlockSpec((None, tm, D_WIDTH), lambda b, t: (b, t, 0)),
            t_spec,
        ],
        out_shape=[
            jax.ShapeDtypeStruct((bsz, s, C_WIDTH), BF16),
            jax.ShapeDtypeStruct((bsz, nb, D_WIDTH, MOBA_BLOCK), BF16),
            jax.ShapeDtypeStruct((bsz, s, D_WIDTH), BF16),
            jax.ShapeDtypeStruct((bsz, nb, D_WIDTH, MOBA_BLOCK), BF16),
        ],
        scratch_shapes=[
            pltpu.VMEM((CONV_HALO, C_WIDTH), F32),
            pltpu.VMEM((CONV_HALO + tm, C_WIDTH), F32),
        ],
        compiler_params=pltpu.CompilerParams(
            dimension_semantics=("arbitrary", "arbitrary"),
            vmem_limit_bytes=VMEM_LIMIT),
        name="mixer_cd_in",
    )(x, mod, g, positions.reshape(bsz, s, 1), invf, w_in.astype(BF16), conv_w)
    return yc, qt, k.reshape(bsz, nb, MOBA_BLOCK, D_WIDTH), vt


def _moba_kernel(qt_ref, k_ref, vt_ref, o_ref, kmean_ref, sel_ref):
    qi = pl.program_id(2)
    nb = k_ref.shape[0]
    tq = qt_ref.shape[1]
    neg_inf = -jnp.inf

    @pl.when(qi == 0)
    def _():
        for n in range(nb):
            kmean_ref[n:n + 1, :] = (jnp.sum(k_ref[n].astype(F32), axis=0, keepdims=True)
                                     * (1.0 / MOBA_BLOCK))

    qt = qt_ref[...]
    dim_head = lax.broadcasted_iota(jnp.int32, (2 * HEAD_DIM, 1), 0) // HEAD_DIM
    lane_head = lax.broadcasted_iota(jnp.int32, (1, 2 * HEAD_DIM), 1) // HEAD_DIM
    blk = lax.broadcasted_iota(jnp.int32, (nb, tq), 0)
    kpos = lax.broadcasted_iota(jnp.int32, (MOBA_BLOCK, tq), 0)
    qpos = lax.broadcasted_iota(jnp.int32, (MOBA_BLOCK, tq), 1)
    causal = kpos <= qpos
    valid = blk < qi

    head_out = []
    for hd in range(2):
        qm = jnp.where(dim_head == hd, qt, jnp.zeros_like(qt))
        kmh = jnp.where(lane_head == hd, kmean_ref[...], 0.0)
        gate = jnp.dot(kmh, qm.astype(F32), preferred_element_type=F32,
                       precision=lax.Precision.HIGHEST)
        gate = jnp.where(valid, gate, neg_inf)
        rank = jnp.zeros((nb, tq), jnp.int32)
        for m in range(nb):
            gm = gate[m:m + 1, :]
            tie = jnp.where(blk > m, 1, 0)
            rank = rank + jnp.where(gm > gate, 1, jnp.where(gm == gate, tie, 0))
        sel_ref[hd] = jnp.where(valid, jnp.where(rank < MOBA_TOPK, 1.0, 0.0), 0.0)

        qs = qm * (HEAD_DIM ** -0.5)

        st = jnp.where(causal, _dot(k_ref[qi], qs), neg_inf)
        m0 = jnp.max(st, axis=0, keepdims=True)
        p = jnp.exp(st - m0)
        l0 = jnp.sum(p, axis=0, keepdims=True)
        acc0 = _dot(vt_ref[qi], p.astype(BF16))

        def body(n, carry, hd=hd, qs=qs):
            m_run, l_run, acc = carry
            st = _dot(k_ref[n], qs)
            st = jnp.where(sel_ref[hd, pl.ds(n, 1), :] > 0.5, st, neg_inf)
            m_new = jnp.maximum(m_run, jnp.max(st, axis=0, keepdims=True))
            alpha = jnp.exp(m_run - m_new)
            p = jnp.exp(st - m_new)
            l_new = alpha * l_run + jnp.sum(p, axis=0, keepdims=True)
            acc = alpha * acc + _dot(vt_ref[n], p.astype(BF16))
            return m_new, l_new, acc

        _, l_fin, acc = lax.fori_loop(0, qi, body, (m0, l0, acc0))
        head_out.append(acc * (1.0 / l_fin))

    out_t = jnp.where(dim_head == 0, head_out[0], head_out[1])
    o_ref[...] = out_t.T.astype(BF16)


def _moba_attention(qt, k, vt):
    bsz, nb, dw, blk = qt.shape
    pairs = dw // (2 * HEAD_DIM)
    s = nb * blk
    return pl.pallas_call(
        _moba_kernel,
        grid=(bsz, pairs, nb),
        in_specs=[
            pl.BlockSpec((None, None, 2 * HEAD_DIM, blk), lambda b, p, i: (b, i, p, 0)),
            pl.BlockSpec((None, nb, blk, 2 * HEAD_DIM), lambda b, p, i: (b, 0, 0, p)),
            pl.BlockSpec((None, nb, 2 * HEAD_DIM, blk), lambda b, p, i: (b, 0, p, 0)),
        ],
        out_specs=pl.BlockSpec((None, blk, 2 * HEAD_DIM), lambda b, p, i: (b, i, p)),
        out_shape=jax.ShapeDtypeStruct((bsz, s, dw), BF16),
        scratch_shapes=[
            pltpu.VMEM((nb, 2 * HEAD_DIM), F32),
            pltpu.VMEM((2, nb, blk), F32),
        ],
        compiler_params=pltpu.CompilerParams(
            dimension_semantics=("arbitrary", "arbitrary", "arbitrary"),
            vmem_limit_bytes=VMEM_LIMIT),
        name="moba_attention",
    )(qt, k, vt)


def kernel(x, c, positions, ab_w_in, ab_vnorm_g, ab_vnorm_b, ab_spatial_w, ab_spatial_b,
           ab_pool_w, ab_pool_scale, ab_w_out, cd_w_in, cd_conv_w, cd_w_out,
           ffn_w_up, ffn_conv_w, ffn_w_down, ada_w, ada_b, norm_g):
    depth = ada_w.shape[0]
    mod = _ada_mod(c, ada_w, ada_b)
    for i in range(depth):
        j = i // 2
        if i % 2 == 0:
            x = _mixer_ab(x, mod[i], norm_g[i], ab_w_in[j], ab_vnorm_g[j], ab_vnorm_b[j],
                          ab_spatial_w[j], ab_spatial_b[j], ab_pool_w[j], ab_pool_scale[j],
                          ab_w_out[j])
            x = _ffn(x, mod[i], norm_g[i], ffn_w_up[i], ffn_conv_w[i], ffn_w_down[i])
        else:
            yc, qt, k, vt = _mixer_cd_in(x, mod[i], norm_g[i], positions, cd_w_in[j], cd_conv_w[j])
            yd = _moba_attention(qt, k, vt)
            x = _ffn(x, mod[i], norm_g[i], ffn_w_up[i], ffn_conv_w[i], ffn_w_down[i],
                     mixer_out=(yc, yd, cd_w_out[j]))
    return x
```

```python
import functools

import jax
import jax.numpy as jnp
from jax import lax
from jax.experimental import pallas as pl
from jax.experimental.pallas import tpu as pltpu

F32 = jnp.float32
BF16 = jnp.bfloat16

D_MODEL = 1024
CHUNK = 128
A_GROUPS = 8
A_WIDTH = 1024
POOL_WINDOWS = (2, 4, 8, 16)
B_GROUP_DIM = 128
B_WIDTH = len(POOL_WINDOWS) * B_GROUP_DIM
POOL_HALO = 16
C_WIDTH = 512
CONV_WIDTH = 3
CONV_HALO = 8
D_HEADS = 8
HEAD_DIM = 64
D_WIDTH = D_HEADS * HEAD_DIM
ROT_DIM = HEAD_DIM // 4
ROPE_THETA = 500000.0
MOBA_BLOCK = 256
MOBA_TOPK = 3
D_FF = 2816
EPS = 1e-6

LANES = 128
TOKEN_TILE = 256
FF_CHUNK = 256
KV_GROUP = 4
VMEM_LIMIT = 56 * 1024 * 1024

SH1, SC1, GT1, SH2, SC2, GT2 = range(6)


def _dot(a, b):
    return jnp.dot(a, b, preferred_element_type=F32)


def _rms(x):
    return x * lax.rsqrt(jnp.mean(x * x, axis=-1, keepdims=True) + EPS)


def _gelu_tanh(x):
    inner = 0.7978845608028654 * (x + 0.044715 * (x * x * x))
    return x * (0.5 * (1.0 + jnp.tanh(inner)))


def _sigmoid(x):
    return 1.0 / (1.0 + jnp.exp(-x))


def _modulated_norm(x, gain, shift, scale):
    return _rms(x) * (gain * (1.0 + scale)) + shift


def _causal_conv3(work_ref, halo_ref, cols, cur, w):
    rows = cur.shape[0]
    work_ref[0:CONV_HALO, :] = halo_ref[:, cols]
    work_ref[CONV_HALO:CONV_HALO + rows, :] = cur
    out = (w[0:1] * work_ref[CONV_HALO - 2:CONV_HALO - 2 + rows, :]
           + w[1:2] * work_ref[CONV_HALO - 1:CONV_HALO - 1 + rows, :]
           + w[2:3] * cur)
    halo_ref[:, cols] = work_ref[rows:rows + CONV_HALO, :]
    return out


def _mod_kernel(c_ref, w_ref, b_ref, o_ref):
    c = c_ref[...]
    act = c * _sigmoid(c)
    o_ref[...] = jnp.dot(act, w_ref[...], preferred_element_type=F32,
                         precision=lax.Precision.HIGHEST) + b_ref[...]


def _ada_mod(c, ada_w, ada_b):
    depth, d, n = ada_w.shape
    bsz = c.shape[0]
    rows = 8
    c_pad = jnp.pad(c, ((0, rows - bsz), (0, 0)))
    tn = 1536
    out = pl.pallas_call(
        _mod_kernel,
        grid=(depth, n // tn),
        in_specs=[
            pl.BlockSpec((rows, d), lambda l, j: (0, 0)),
            pl.BlockSpec((None, d, tn), lambda l, j: (l, 0, j)),
            pl.BlockSpec((None, 1, tn), lambda l, j: (l, 0, j)),
        ],
        out_specs=pl.BlockSpec((None, rows, tn), lambda l, j: (l, 0, j)),
        out_shape=jax.ShapeDtypeStruct((depth, rows, n), F32),
        compiler_params=pltpu.CompilerParams(
            dimension_semantics=("arbitrary", "arbitrary"),
            vmem_limit_bytes=VMEM_LIMIT),
        name="ada_mod",
    )(c_pad, ada_w, ada_b.reshape(depth, 1, n))
    return out[:, :bsz].reshape(depth, bsz, 6, d)


def _mixer_ab_kernel(x_ref, mod_ref, g_ref, win_ref, vg_ref, vb_ref, ws_ref, bsf_ref,
                     pw_ref, ps_ref, wout_ref, o_ref, xb_ref, ycat_ref):
    t = pl.program_id(1)
    tm = x_ref.shape[0]

    @pl.when(t == 0)
    def _():
        xb_ref[0:POOL_HALO, :] = jnp.zeros((POOL_HALO, B_WIDTH), F32)

    x = x_ref[...]
    mod = mod_ref[...]
    g = g_ref[...]
    h = _modulated_norm(x, g[0:1], mod[SH1:SH1 + 1], mod[SC1:SC1 + 1]).astype(BF16)

    gv = _gelu_tanh(_dot(h, win_ref[:, A_WIDTH:2 * A_WIDTH]))
    mu = jnp.mean(gv, axis=-1, keepdims=True)
    dv = gv - mu
    var = jnp.mean(dv * dv, axis=-1, keepdims=True)
    vn = (dv * lax.rsqrt(var + EPS) * vg_ref[...] + vb_ref[...]).astype(BF16)
    gu = _gelu_tanh(_dot(h, win_ref[:, 0:A_WIDTH]))

    row = lax.broadcasted_iota(jnp.int32, (CHUNK, CHUNK), 0)
    col = lax.broadcasted_iota(jnp.int32, (CHUNK, CHUNK), 1)
    tril = row >= col
    for gi in range(A_GROUPS):
        cols = slice(gi * LANES, (gi + 1) * LANES)
        wg = jnp.where(tril, ws_ref[gi], 0.0).astype(BF16)
        bias = bsf_ref[:, cols]
        for ci in range(tm // CHUNK):
            rows = slice(ci * CHUNK, (ci + 1) * CHUNK)
            mixed = _dot(wg, vn[rows, cols]) + bias
            ycat_ref[rows, cols] = (gu[rows, cols] * mixed).astype(BF16)

    xb = _dot(h, win_ref[:, 2 * A_WIDTH:2 * A_WIDTH + B_WIDTH])
    xb_ref[POOL_HALO:POOL_HALO + tm, :] = xb
    pos = t * tm + lax.broadcasted_iota(jnp.int32, (tm, 1), 0)
    for gi, w in enumerate(POOL_WINDOWS):
        cols = slice(gi * LANES, (gi + 1) * LANES)
        s = xb[:, cols]
        for k in range(1, w):
            s = s + xb_ref[POOL_HALO - k:POOL_HALO - k + tm, cols]
        cnt = jnp.minimum(pos + 1, w).astype(F32)
        pooled = s * (1.0 / cnt) - xb[:, cols]
        mixed = _dot(pooled.astype(BF16), pw_ref[gi]) * ps_ref[:, cols]
        ycat_ref[:, A_WIDTH + gi * LANES:A_WIDTH + (gi + 1) * LANES] = mixed.astype(BF16)
    xb_ref[0:POOL_HALO, :] = xb_ref[tm:tm + POOL_HALO, :]

    y = _dot(ycat_ref[...], wout_ref[...])
    o_ref[...] = x + mod[GT1:GT1 + 1] * (_rms(y) * g[1:2])


def _const_spec(shape):
    zeros = (0,) * len(shape)
    return pl.BlockSpec(shape, lambda b, t: zeros, pipeline_mode=pl.Buffered(1))


def _mixer_ab(x, mod, g, w_in, vnorm_g, vnorm_b, w_s, b_s, pool_w, pool_scale, w_out):
    bsz, s, d = x.shape
    tm = TOKEN_TILE
    bias_full = jnp.repeat(b_s.T, LANES, axis=1)
    return pl.pallas_call(
        _mixer_ab_kernel,
        grid=(bsz, s // tm),
        in_specs=[
            pl.BlockSpec((None, tm, d), lambda b, t: (b, t, 0)),
            pl.BlockSpec((None, 6, d), lambda b, t: (b, 0, 0)),
            _const_spec((4, d)),
            _const_spec(w_in.shape),
            _const_spec((1, A_WIDTH)),
            _const_spec((1, A_WIDTH)),
            _const_spec(w_s.shape),
            _const_spec(bias_full.shape),
            _const_spec(pool_w.shape),
            _const_spec((1, B_WIDTH)),
            _const_spec(w_out.shape),
        ],
        out_specs=pl.BlockSpec((None, tm, d), lambda b, t: (b, t, 0)),
        out_shape=jax.ShapeDtypeStruct(x.shape, F32),
        scratch_shapes=[
            pltpu.VMEM((POOL_HALO + tm, B_WIDTH), F32),
            pltpu.VMEM((tm, A_WIDTH + B_WIDTH), BF16),
        ],
        compiler_params=pltpu.CompilerParams(
            dimension_semantics=("arbitrary", "arbitrary"),
            vmem_limit_bytes=VMEM_LIMIT),
        name="mixer_ab",
    )(x, mod, g, w_in.astype(BF16), vnorm_g.reshape(1, -1), vnorm_b.reshape(1, -1),
      w_s, bias_full, pool_w.astype(BF16), pool_scale.reshape(1, -1), w_out.astype(BF16))


def _ffn_kernel(fuse_mixer_out, *refs):
    if fuse_mixer_out:
        (x_ref, yc_ref, yd_ref, wo_ref, mod_ref, g_ref, wup_ref, cw_ref, wdn_ref,
         o_ref, halo_ref, work_ref, act_ref) = refs
    else:
        (x_ref, mod_ref, g_ref, wup_ref, cw_ref, wdn_ref,
         o_ref, halo_ref, work_ref, act_ref) = refs
    t = pl.program_id(1)

    @pl.when(t == 0)
    def _():
        halo_ref[...] = jnp.zeros(halo_ref.shape, F32)

    x = x_ref[...]
    mod = mod_ref[...]
    g = g_ref[...]
    if fuse_mixer_out:
        y = _dot(yc_ref[...], wo_ref[0:C_WIDTH, :]) + _dot(yd_ref[...], wo_ref[C_WIDTH:, :])
        x = x + mod[GT1:GT1 + 1] * (_rms(y) * g[1:2])
    h = _modulated_norm(x, g[2:3], mod[SH2:SH2 + 1], mod[SC2:SC2 + 1]).astype(BF16)

    for j in range(D_FF // FF_CHUNK):
        halves = []
        for half in range(2):
            start = half * D_FF + j * FF_CHUNK
            cols = slice(start, start + FF_CHUNK)
            up = _dot(h, wup_ref[:, cols])
            halves.append(_causal_conv3(work_ref.at[half], halo_ref, cols, up, cw_ref[:, cols]))
        gate, lin = halves
        act = gate * _sigmoid(gate) * lin
        act_ref[:, j * FF_CHUNK:(j + 1) * FF_CHUNK] = act.astype(BF16)

    y = _dot(act_ref[...], wdn_ref[...])
    o_ref[...] = x + mod[GT2:GT2 + 1] * (_rms(y) * g[3:4])


def _ffn(x, mod, g, w_up, conv_w, w_down, mixer_out=None):
    bsz, s, d = x.shape
    tm = TOKEN_TILE
    tile = lambda width: pl.BlockSpec((None, tm, width), lambda b, t: (b, t, 0))
    in_specs = [tile(d)]
    args = [x]
    if mixer_out is not None:
        yc, yd, w_o = mixer_out
        in_specs += [tile(C_WIDTH), tile(D_WIDTH), _const_spec(w_o.shape)]
        args += [yc, yd, w_o.astype(BF16)]
    in_specs += [
        pl.BlockSpec((None, 6, d), lambda b, t: (b, 0, 0)),
        _const_spec((4, d)),
        _const_spec(w_up.shape),
        _const_spec(conv_w.shape),
        _const_spec(w_down.shape),
    ]
    args += [mod, g, w_up.astype(BF16), conv_w, w_down.astype(BF16)]
    return pl.pallas_call(
        functools.partial(_ffn_kernel, mixer_out is not None),
        grid=(bsz, s // tm),
        in_specs=in_specs,
        out_specs=tile(d),
        out_shape=jax.ShapeDtypeStruct(x.shape, F32),
        scratch_shapes=[
            pltpu.VMEM((CONV_HALO, 2 * D_FF), F32),
            pltpu.VMEM((2, CONV_HALO + tm, FF_CHUNK), F32),
            pltpu.VMEM((tm, D_FF), BF16),
        ],
        compiler_params=pltpu.CompilerParams(
            dimension_semantics=("arbitrary", "arbitrary"),
            vmem_limit_bytes=VMEM_LIMIT),
        name="conv_ffn_fused" if mixer_out is not None else "conv_ffn",
    )(*args)


def _mixer_cd_in_kernel(x_ref, mod_ref, g_ref, pos_ref, invf_ref, win_ref, cw_ref,
                        yc_ref, qt_ref, k_ref, vt_ref, halo_ref, work_ref):
    t = pl.program_id(1)
    tm = x_ref.shape[0]

    @pl.when(t == 0)
    def _():
        halo_ref[...] = jnp.zeros(halo_ref.shape, F32)

    mod = mod_ref[...]
    g = g_ref[...]
    h = _modulated_norm(x_ref[...], g[0:1], mod[SH1:SH1 + 1], mod[SC1:SC1 + 1]).astype(BF16)

    def proj(i):
        return _dot(h, win_ref[:, i * C_WIDTH:(i + 1) * C_WIDTH])

    prod = proj(1) * proj(2)
    conv = _causal_conv3(work_ref, halo_ref, slice(0, C_WIDTH), prod, cw_ref[...])
    yc_ref[...] = (proj(0) * conv).astype(BF16)

    ang = pos_ref[...].astype(F32) * invf_ref[...]
    lane = lax.broadcasted_iota(jnp.int32, (1, LANES), 1) % HEAD_DIM
    half = ROT_DIM // 2
    cos = jnp.where(lane < ROT_DIM, jnp.cos(ang), 1.0)
    sin = jnp.sin(ang)
    sin = jnp.where(lane < half, -sin, jnp.where(lane < ROT_DIM, sin, 0.0))

    def rope(v):
        outs = []
        for ci in range(v.shape[1] // LANES):
            vc = v[:, ci * LANES:(ci + 1) * LANES]
            partner = jnp.where(lane < half, pltpu.roll(vc, LANES - half, 1),
                                pltpu.roll(vc, half, 1))
            outs.append(vc * cos + partner * sin)
        return jnp.concatenate(outs, axis=1)

    q = rope(proj(3))
    k = rope(proj(4))
    v = proj(5)
    for bi in range(tm // MOBA_BLOCK):
        rows = slice(bi * MOBA_BLOCK, (bi + 1) * MOBA_BLOCK)
        qt_ref[bi] = q[rows].T.astype(BF16)
        vt_ref[bi] = v[rows].T.astype(BF16)
    k_ref[...] = k.astype(BF16)


def _mixer_cd_in(x, mod, g, positions, w_in, conv_w):
    bsz, s, d = x.shape
    tm = TOKEN_TILE
    nb = s // MOBA_BLOCK
    bpt = tm // MOBA_BLOCK
    lane = jnp.arange(LANES) % HEAD_DIM
    inv_freq = ROPE_THETA ** (-jnp.arange(0, ROT_DIM, 2, dtype=F32) / ROT_DIM)
    invf = jnp.where(lane < ROT_DIM, inv_freq[lane % (ROT_DIM // 2)], 0.0).reshape(1, LANES)
    t_spec = pl.BlockSpec((None, bpt, D_WIDTH, MOBA_BLOCK), lambda b, t: (b, t, 0, 0))
    yc, qt, k, vt = pl.pallas_call(
        _mixer_cd_in_kernel,
        grid=(bsz, s // tm),
        in_specs=[
            pl.BlockSpec((None, tm, d), lambda b, t: (b, t, 0)),
            pl.BlockSpec((None, 6, d), lambda b, t: (b, 0, 0)),
            _const_spec((4, d)),
            pl.BlockSpec((None, tm, 1), lambda b, t: (b, t, 0)),
            _const_spec((1, LANES)),
            _const_spec(w_in.shape),
            _const_spec(conv_w.shape),
        ],
        out_specs=[
            pl.BlockSpec((None, tm, C_WIDTH), lambda b, t: (b, t, 0)),
            t_spec,
            pl.BlockSpec((None, tm, D_WIDTH), lambda b, t: (b, t, 0)),
            t_spec,
        ],
        out_shape=[
            jax.ShapeDtypeStruct((bsz, s, C_WIDTH), BF16),
            jax.ShapeDtypeStruct((bsz, nb, D_WIDTH, MOBA_BLOCK), BF16),
            jax.ShapeDtypeStruct((bsz, s, D_WIDTH), BF16),
            jax.ShapeDtypeStruct((bsz, nb, D_WIDTH, MOBA_BLOCK), BF16),
        ],
        scratch_shapes=[
            pltpu.VMEM((CONV_HALO, C_WIDTH), F32),
            pltpu.VMEM((CONV_HALO + tm, C_WIDTH), F32),
        ],
        compiler_params=pltpu.CompilerParams(
            dimension_semantics=("arbitrary", "arbitrary"),
            vmem_limit_bytes=VMEM_LIMIT),
        name="mixer_cd_in",
    )(x, mod, g, positions.reshape(bsz, s, 1), invf, w_in.astype(BF16), conv_w)
    return yc, qt, k.reshape(bsz, nb, MOBA_BLOCK, D_WIDTH), vt


def _moba_kernel(qt_ref, k_ref, vt_ref, o_ref, kmean_ref, sel_ref):
    qi = pl.program_id(2)
    nb = k_ref.shape[0]
    tq = qt_ref.shape[1]
    neg_inf = -jnp.inf

    @pl.when(qi == 0)
    def _():
        for n in range(nb):
            kmean_ref[n:n + 1, :] = (jnp.sum(k_ref[n].astype(F32), axis=0, keepdims=True)
                                     * (1.0 / MOBA_BLOCK))

    qt = qt_ref[...]
    dim_head = lax.broadcasted_iota(jnp.int32, (2 * HEAD_DIM, 1), 0) // HEAD_DIM
    lane_head = lax.broadcasted_iota(jnp.int32, (1, 2 * HEAD_DIM), 1) // HEAD_DIM
    blk = lax.broadcasted_iota(jnp.int32, (nb, tq), 0)
    kpos = lax.broadcasted_iota(jnp.int32, (MOBA_BLOCK, tq), 0)
    qpos = lax.broadcasted_iota(jnp.int32, (MOBA_BLOCK, tq), 1)
    causal = kpos <= qpos
    valid = blk < qi

    qs, chains = [], []
    for hd in range(2):
        qm = jnp.where(dim_head == hd, qt, jnp.zeros_like(qt))
        kmh = jnp.where(lane_head == hd, kmean_ref[...], 0.0)
        gate = jnp.dot(kmh, qm.astype(F32), preferred_element_type=F32,
                       precision=lax.Precision.HIGHEST)
        gate = jnp.where(valid, gate, neg_inf)
        rank = jnp.zeros((nb, tq), jnp.int32)
        for m in range(nb):
            gm = gate[m:m + 1, :]
            tie = jnp.where(blk > m, 1, 0)
            rank = rank + jnp.where(gm > gate, 1, jnp.where(gm == gate, tie, 0))
        sel_ref[hd] = jnp.where(valid, jnp.where(rank < MOBA_TOPK, 1.0, 0.0), 0.0)

        qs.append(qm * (HEAD_DIM ** -0.5))

        st = jnp.where(causal, _dot(k_ref[qi], qs[hd]), neg_inf)
        m0 = jnp.max(st, axis=0, keepdims=True)
        p = jnp.exp(st - m0)
        l0 = jnp.sum(p, axis=0, keepdims=True)
        chains.append((m0, l0, _dot(vt_ref[qi], p.astype(BF16))))

    def group(gi, carry):
        out = []
        for hd in range(2):
            m_run, l_run, acc = carry[hd]
            sts, picked = [], []
            m_new = m_run
            for u in range(KV_GROUP):
                n = gi * KV_GROUP + u
                st = _dot(k_ref[n], qs[hd])
                pick = sel_ref[hd, pl.ds(n, 1), :] > 0.5
                m_new = jnp.maximum(m_new, jnp.where(pick, jnp.max(st, axis=0, keepdims=True),
                                                     neg_inf))
                sts.append(st)
                picked.append(pick)
            alpha = jnp.exp(m_run - m_new)
            l_new = alpha * l_run
            acc = alpha * acc
            for u in range(KV_GROUP):
                p = jnp.exp(sts[u] - jnp.where(picked[u], m_new, jnp.inf))
                l_new = l_new + jnp.sum(p, axis=0, keepdims=True)
                acc = acc + _dot(vt_ref[gi * KV_GROUP + u], p.astype(BF16))
            out.append((m_new, l_new, acc))
        return tuple(out)

    n_groups = (qi + KV_GROUP - 1) // KV_GROUP
    chains = lax.fori_loop(0, n_groups, group, tuple(chains))
    head_out = [acc * (1.0 / l_fin) for _, l_fin, acc in chains]

    out_t = jnp.where(dim_head == 0, head_out[0], head_out[1])
    o_ref[...] = out_t.T.astype(BF16)


def _moba_attention(qt, k, vt):
    bsz, nb, dw, blk = qt.shape
    pairs = dw // (2 * HEAD_DIM)
    s = nb * blk
    return pl.pallas_call(
        _moba_kernel,
        grid=(bsz, pairs, nb),
        in_specs=[
            pl.BlockSpec((None, None, 2 * HEAD_DIM, blk), lambda b, p, i: (b, i, p, 0)),
            pl.BlockSpec((None, nb, blk, 2 * HEAD_DIM), lambda b, p, i: (b, 0, 0, p)),
            pl.BlockSpec((None, nb, 2 * HEAD_DIM, blk), lambda b, p, i: (b, 0, p, 0)),
        ],
        out_specs=pl.BlockSpec((None, blk, 2 * HEAD_DIM), lambda b, p, i: (b, i, p)),
        out_shape=jax.ShapeDtypeStruct((bsz, s, dw), BF16),
        scratch_shapes=[
            pltpu.VMEM((nb, 2 * HEAD_DIM), F32),
            pltpu.VMEM((2, nb, blk), F32),
        ],
        compiler_params=pltpu.CompilerParams(
            dimension_semantics=("arbitrary", "arbitrary", "arbitrary"),
            vmem_limit_bytes=VMEM_LIMIT),
        name="moba_attention",
    )(qt, k, vt)


def kernel(x, c, positions, ab_w_in, ab_vnorm_g, ab_vnorm_b, ab_spatial_w, ab_spatial_b,
           ab_pool_w, ab_pool_scale, ab_w_out, cd_w_in, cd_conv_w, cd_w_out,
           ffn_w_up, ffn_conv_w, ffn_w_down, ada_w, ada_b, norm_g):
    depth = ada_w.shape[0]
    mod = _ada_mod(c, ada_w, ada_b)
    for i in range(depth):
        j = i // 2
        if i % 2 == 0:
            x = _mixer_ab(x, mod[i], norm_g[i], ab_w_in[j], ab_vnorm_g[j], ab_vnorm_b[j],
                          ab_spatial_w[j], ab_spatial_b[j], ab_pool_w[j], ab_pool_scale[j],
                          ab_w_out[j])
            x = _ffn(x, mod[i], norm_g[i], ffn_w_up[i], ffn_conv_w[i], ffn_w_down[i])
        else:
            yc, qt, k, vt = _mixer_cd_in(x, mod[i], norm_g[i], positions, cd_w_in[j], cd_conv_w[j])
            yd = _moba_attention(qt, k, vt)
            x = _ffn(x, mod[i], norm_g[i], ffn_w_up[i], ffn_conv_w[i], ffn_w_down[i],
                     mixer_out=(yc, yd, cd_w_out[j]))
    return x
```

```python
import functools

import jax
import jax.numpy as jnp
from jax import lax
from jax.experimental import pallas as pl
from jax.experimental.pallas import tpu as pltpu

F32 = jnp.float32
BF16 = jnp.bfloat16

D_MODEL = 1024
CHUNK = 128
A_GROUPS = 8
A_WIDTH = 1024
POOL_WINDOWS = (2, 4, 8, 16)
B_GROUP_DIM = 128
B_WIDTH = len(POOL_WINDOWS) * B_GROUP_DIM
POOL_HALO = 16
C_WIDTH = 512
CONV_WIDTH = 3
CONV_HALO = 8
D_HEADS = 8
HEAD_DIM = 64
D_WIDTH = D_HEADS * HEAD_DIM
ROT_DIM = HEAD_DIM // 4
ROPE_THETA = 500000.0
MOBA_BLOCK = 256
MOBA_TOPK = 3
D_FF = 2816
EPS = 1e-6

LANES = 128
TOKEN_TILE = 256
FF_CHUNK = 256
KV_GROUP = 4
HEADS_PER_STEP = 4
VT_ROWS = HEAD_DIM + 16
QK_SCALE = HEAD_DIM ** -0.5 * 1.4426950408889634
VMEM_LIMIT = 56 * 1024 * 1024

SH1, SC1, GT1, SH2, SC2, GT2 = range(6)


def _dot(a, b):
    return jnp.dot(a, b, preferred_element_type=F32)


def _rms(x):
    return x * lax.rsqrt(jnp.mean(x * x, axis=-1, keepdims=True) + EPS)


def _gelu_tanh(x):
    inner = 0.7978845608028654 * (x + 0.044715 * (x * x * x))
    return x * (0.5 * (1.0 + jnp.tanh(inner)))


def _sigmoid(x):
    return 1.0 / (1.0 + jnp.exp(-x))


def _modulated_norm(x, gain, shift, scale):
    return _rms(x) * (gain * (1.0 + scale)) + shift


def _causal_conv3(work_ref, halo_ref, cols, cur, w):
    rows = cur.shape[0]
    work_ref[0:CONV_HALO, :] = halo_ref[:, cols]
    work_ref[CONV_HALO:CONV_HALO + rows, :] = cur
    out = (w[0:1] * work_ref[CONV_HALO - 2:CONV_HALO - 2 + rows, :]
           + w[1:2] * work_ref[CONV_HALO - 1:CONV_HALO - 1 + rows, :]
           + w[2:3] * cur)
    halo_ref[:, cols] = work_ref[rows:rows + CONV_HALO, :]
    return out


def _mod_kernel(c_ref, w_ref, b_ref, o_ref):
    c = c_ref[...]
    act = c * _sigmoid(c)
    o_ref[...] = jnp.dot(act, w_ref[...], preferred_element_type=F32,
                         precision=lax.Precision.HIGHEST) + b_ref[...]


def _ada_mod(c, ada_w, ada_b):
    depth, d, n = ada_w.shape
    bsz = c.shape[0]
    rows = 8
    c_pad = jnp.pad(c, ((0, rows - bsz), (0, 0)))
    tn = 1536
    out = pl.pallas_call(
        _mod_kernel,
        grid=(depth, n // tn),
        in_specs=[
            pl.BlockSpec((rows, d), lambda l, j: (0, 0)),
            pl.BlockSpec((None, d, tn), lambda l, j: (l, 0, j)),
            pl.BlockSpec((None, 1, tn), lambda l, j: (l, 0, j)),
        ],
        out_specs=pl.BlockSpec((None, rows, tn), lambda l, j: (l, 0, j)),
        out_shape=jax.ShapeDtypeStruct((depth, rows, n), F32),
        compiler_params=pltpu.CompilerParams(
            dimension_semantics=("arbitrary", "arbitrary"),
            vmem_limit_bytes=VMEM_LIMIT),
        name="ada_mod",
    )(c_pad, ada_w, ada_b.reshape(depth, 1, n))
    return out[:, :bsz].reshape(depth, bsz, 6, d)


def _mixer_ab_kernel(x_ref, mod_ref, g_ref, win_ref, vg_ref, vb_ref, ws_ref, bsf_ref,
                     pw_ref, ps_ref, wout_ref, o_ref, xb_ref, ycat_ref):
    t = pl.program_id(1)
    tm = x_ref.shape[0]

    @pl.when(t == 0)
    def _():
        xb_ref[0:POOL_HALO, :] = jnp.zeros((POOL_HALO, B_WIDTH), F32)

    x = x_ref[...]
    mod = mod_ref[...]
    g = g_ref[...]
    h = _modulated_norm(x, g[0:1], mod[SH1:SH1 + 1], mod[SC1:SC1 + 1]).astype(BF16)

    gv = _gelu_tanh(_dot(h, win_ref[:, A_WIDTH:2 * A_WIDTH]))
    mu = jnp.mean(gv, axis=-1, keepdims=True)
    dv = gv - mu
    var = jnp.mean(dv * dv, axis=-1, keepdims=True)
    vn = (dv * lax.rsqrt(var + EPS) * vg_ref[...] + vb_ref[...]).astype(BF16)
    gu = _gelu_tanh(_dot(h, win_ref[:, 0:A_WIDTH]))

    row = lax.broadcasted_iota(jnp.int32, (CHUNK, CHUNK), 0)
    col = lax.broadcasted_iota(jnp.int32, (CHUNK, CHUNK), 1)
    tril = row >= col
    for gi in range(A_GROUPS):
        cols = slice(gi * LANES, (gi + 1) * LANES)
        wg = jnp.where(tril, ws_ref[gi], 0.0).astype(BF16)
        bias = bsf_ref[:, cols]
        for ci in range(tm // CHUNK):
            rows = slice(ci * CHUNK, (ci + 1) * CHUNK)
            mixed = _dot(wg, vn[rows, cols]) + bias
            ycat_ref[rows, cols] = (gu[rows, cols] * mixed).astype(BF16)

    xb = _dot(h, win_ref[:, 2 * A_WIDTH:2 * A_WIDTH + B_WIDTH])
    xb_ref[POOL_HALO:POOL_HALO + tm, :] = xb
    pos = t * tm + lax.broadcasted_iota(jnp.int32, (tm, 1), 0)
    for gi, w in enumerate(POOL_WINDOWS):
        cols = slice(gi * LANES, (gi + 1) * LANES)
        s = xb[:, cols]
        for k in range(1, w):
            s = s + xb_ref[POOL_HALO - k:POOL_HALO - k + tm, cols]
        cnt = jnp.minimum(pos + 1, w).astype(F32)
        pooled = s * (1.0 / cnt) - xb[:, cols]
        mixed = _dot(pooled.astype(BF16), pw_ref[gi]) * ps_ref[:, cols]
        ycat_ref[:, A_WIDTH + gi * LANES:A_WIDTH + (gi + 1) * LANES] = mixed.astype(BF16)
    xb_ref[0:POOL_HALO, :] = xb_ref[tm:tm + POOL_HALO, :]

    y = _dot(ycat_ref[...], wout_ref[...])
    o_ref[...] = x + mod[GT1:GT1 + 1] * (_rms(y) * g[1:2])


def _const_spec(shape):
    zeros = (0,) * len(shape)
    return pl.BlockSpec(shape, lambda b, t: zeros, pipeline_mode=pl.Buffered(1))


def _mixer_ab(x, mod, g, w_in, vnorm_g, vnorm_b, w_s, b_s, pool_w, pool_scale, w_out):
    bsz, s, d = x.shape
    tm = TOKEN_TILE
    bias_full = jnp.repeat(b_s.T, LANES, axis=1)
    return pl.pallas_call(
        _mixer_ab_kernel,
        grid=(bsz, s // tm),
        in_specs=[
            pl.BlockSpec((None, tm, d), lambda b, t: (b, t, 0)),
            pl.BlockSpec((None, 6, d), lambda b, t: (b, 0, 0)),
            _const_spec((4, d)),
            _const_spec(w_in.shape),
            _const_spec((1, A_WIDTH)),
            _const_spec((1, A_WIDTH)),
            _const_spec(w_s.shape),
            _const_spec(bias_full.shape),
            _const_spec(pool_w.shape),
            _const_spec((1, B_WIDTH)),
            _const_spec(w_out.shape),
        ],
        out_specs=pl.BlockSpec((None, tm, d), lambda b, t: (b, t, 0)),
        out_shape=jax.ShapeDtypeStruct(x.shape, F32),
        scratch_shapes=[
            pltpu.VMEM((POOL_HALO + tm, B_WIDTH), F32),
            pltpu.VMEM((tm, A_WIDTH + B_WIDTH), BF16),
        ],
        compiler_params=pltpu.CompilerParams(
            dimension_semantics=("arbitrary", "arbitrary"),
            vmem_limit_bytes=VMEM_LIMIT),
        name="mixer_ab",
    )(x, mod, g, w_in.astype(BF16), vnorm_g.reshape(1, -1), vnorm_b.reshape(1, -1),
      w_s, bias_full, pool_w.astype(BF16), pool_scale.reshape(1, -1), w_out.astype(BF16))


def _ffn_kernel(fuse_mixer_out, *refs):
    if fuse_mixer_out:
        (x_ref, yc_ref, yd_ref, wo_ref, mod_ref, g_ref, wup_ref, cw_ref, wdn_ref,
         o_ref, halo_ref, work_ref, act_ref) = refs
    else:
        (x_ref, mod_ref, g_ref, wup_ref, cw_ref, wdn_ref,
         o_ref, halo_ref, work_ref, act_ref) = refs
    t = pl.program_id(1)

    @pl.when(t == 0)
    def _():
        halo_ref[...] = jnp.zeros(halo_ref.shape, F32)

    x = x_ref[...]
    mod = mod_ref[...]
    g = g_ref[...]
    if fuse_mixer_out:
        y = _dot(yc_ref[...], wo_ref[0:C_WIDTH, :]) + _dot(yd_ref[...], wo_ref[C_WIDTH:, :])
        x = x + mod[GT1:GT1 + 1] * (_rms(y) * g[1:2])
    h = _modulated_norm(x, g[2:3], mod[SH2:SH2 + 1], mod[SC2:SC2 + 1]).astype(BF16)

    for j in range(D_FF // FF_CHUNK):
        halves = []
        for half in range(2):
            start = half * D_FF + j * FF_CHUNK
            cols = slice(start, start + FF_CHUNK)
            up = _dot(h, wup_ref[:, cols])
            halves.append(_causal_conv3(work_ref.at[half], halo_ref, cols, up, cw_ref[:, cols]))
        gate, lin = halves
        act = gate * _sigmoid(gate) * lin
        act_ref[:, j * FF_CHUNK:(j + 1) * FF_CHUNK] = act.astype(BF16)

    y = _dot(act_ref[...], wdn_ref[...])
    o_ref[...] = x + mod[GT2:GT2 + 1] * (_rms(y) * g[3:4])


def _ffn(x, mod, g, w_up, conv_w, w_down, mixer_out=None):
    bsz, s, d = x.shape
    tm = TOKEN_TILE
    tile = lambda width: pl.BlockSpec((None, tm, width), lambda b, t: (b, t, 0))
    in_specs = [tile(d)]
    args = [x]
    if mixer_out is not None:
        yc, yd, w_o = mixer_out
        in_specs += [tile(C_WIDTH), tile(D_WIDTH), _const_spec(w_o.shape)]
        args += [yc, yd, w_o.astype(BF16)]
    in_specs += [
        pl.BlockSpec((None, 6, d), lambda b, t: (b, 0, 0)),
        _const_spec((4, d)),
        _const_spec(w_up.shape),
        _const_spec(conv_w.shape),
        _const_spec(w_down.shape),
    ]
    args += [mod, g, w_up.astype(BF16), conv_w, w_down.astype(BF16)]
    return pl.pallas_call(
        functools.partial(_ffn_kernel, mixer_out is not None),
        grid=(bsz, s // tm),
        in_specs=in_specs,
        out_specs=tile(d),
        out_shape=jax.ShapeDtypeStruct(x.shape, F32),
        scratch_shapes=[
            pltpu.VMEM((CONV_HALO, 2 * D_FF), F32),
            pltpu.VMEM((2, CONV_HALO + tm, FF_CHUNK), F32),
            pltpu.VMEM((tm, D_FF), BF16),
        ],
        compiler_params=pltpu.CompilerParams(
            dimension_semantics=("arbitrary", "arbitrary"),
            vmem_limit_bytes=VMEM_LIMIT),
        name="conv_ffn_fused" if mixer_out is not None else "conv_ffn",
    )(*args)


def _mixer_cd_in_kernel(x_ref, mod_ref, g_ref, pos_ref, invf_ref, win_ref, cw_ref,
                        yc_ref, qt_ref, k_ref, vt_ref, halo_ref, work_ref):
    t = pl.program_id(1)
    tm = x_ref.shape[0]

    @pl.when(t == 0)
    def _():
        halo_ref[...] = jnp.zeros(halo_ref.shape, F32)

    mod = mod_ref[...]
    g = g_ref[...]
    h = _modulated_norm(x_ref[...], g[0:1], mod[SH1:SH1 + 1], mod[SC1:SC1 + 1]).astype(BF16)

    def proj(i):
        return _dot(h, win_ref[:, i * C_WIDTH:(i + 1) * C_WIDTH])

    prod = proj(1) * proj(2)
    conv = _causal_conv3(work_ref, halo_ref, slice(0, C_WIDTH), prod, cw_ref[...])
    yc_ref[...] = (proj(0) * conv).astype(BF16)

    ang = pos_ref[...].astype(F32) * invf_ref[...]
    lane = lax.broadcasted_iota(jnp.int32, (1, LANES), 1) % HEAD_DIM
    half = ROT_DIM // 2
    cos = jnp.where(lane < ROT_DIM, jnp.cos(ang), 1.0)
    sin = jnp.sin(ang)
    sin = jnp.where(lane < half, -sin, jnp.where(lane < ROT_DIM, sin, 0.0))

    def rope(v):
        outs = []
        for ci in range(v.shape[1] // LANES):
            vc = v[:, ci * LANES:(ci + 1) * LANES]
            partner = jnp.where(lane < half, pltpu.roll(vc, LANES - half, 1),
                                pltpu.roll(vc, half, 1))
            outs.append(vc * cos + partner * sin)
        return jnp.concatenate(outs, axis=1)

    q = rope(proj(3)) * QK_SCALE
    k = rope(proj(4))
    v = proj(5)
    ones_rows = jnp.where(
        lax.broadcasted_iota(jnp.int32, (VT_ROWS - HEAD_DIM, MOBA_BLOCK), 0) == 0, 1.0, 0.0)
    for bi in range(tm // MOBA_BLOCK):
        rows = slice(bi * MOBA_BLOCK, (bi + 1) * MOBA_BLOCK)
        qt_ref[bi] = q[rows].T.astype(BF16)
        v_t = v[rows].T
        for hd in range(D_HEADS):
            vt_ref[bi, hd] = jnp.concatenate(
                [v_t[hd * HEAD_DIM:(hd + 1) * HEAD_DIM], ones_rows], axis=0).astype(BF16)
    k_ref[...] = k.astype(BF16)


def _mixer_cd_in(x, mod, g, positions, w_in, conv_w):
    bsz, s, d = x.shape
    tm = TOKEN_TILE
    nb = s // MOBA_BLOCK
    bpt = tm // MOBA_BLOCK
    lane = jnp.arange(LANES) % HEAD_DIM
    inv_freq = ROPE_THETA ** (-jnp.arange(0, ROT_DIM, 2, dtype=F32) / ROT_DIM)
    invf = jnp.where(lane < ROT_DIM, inv_freq[lane % (ROT_DIM // 2)], 0.0).reshape(1, LANES)
    yc, qt, k, vt = pl.pallas_call(
        _mixer_cd_in_kernel,
        grid=(bsz, s // tm),
        in_specs=[
            pl.BlockSpec((None, tm, d), lambda b, t: (b, t, 0)),
            pl.BlockSpec((None, 6, d), lambda b, t: (b, 0, 0)),
            _const_spec((4, d)),
            pl.BlockSpec((None, tm, 1), lambda b, t: (b, t, 0)),
            _const_spec((1, LANES)),
            _const_spec(w_in.shape),
            _const_spec(conv_w.shape),
        ],
        out_specs=[
            pl.BlockSpec((None, tm, C_WIDTH), lambda b, t: (b, t, 0)),
            pl.BlockSpec((None, bpt, D_WIDTH, MOBA_BLOCK), lambda b, t: (b, t, 0, 0)),
            pl.BlockSpec((None, tm, D_WIDTH), lambda b, t: (b, t, 0)),
            pl.BlockSpec((None, bpt, D_HEADS, VT_ROWS, MOBA_BLOCK), lambda b, t: (b, t, 0, 0, 0)),
        ],
        out_shape=[
            jax.ShapeDtypeStruct((bsz, s, C_WIDTH), BF16),
            jax.ShapeDtypeStruct((bsz, nb, D_WIDTH, MOBA_BLOCK), BF16),
            jax.ShapeDtypeStruct((bsz, s, D_WIDTH), BF16),
            jax.ShapeDtypeStruct((bsz, nb, D_HEADS, VT_ROWS, MOBA_BLOCK), BF16),
        ],
        scratch_shapes=[
            pltpu.VMEM((CONV_HALO, C_WIDTH), F32),
            pltpu.VMEM((CONV_HALO + tm, C_WIDTH), F32),
        ],
        compiler_params=pltpu.CompilerParams(
            dimension_semantics=("arbitrary", "arbitrary"),
            vmem_limit_bytes=VMEM_LIMIT),
        name="mixer_cd_in",
    )(x, mod, g, positions.reshape(bsz, s, 1), invf, w_in.astype(BF16), conv_w)
    return yc, qt, k.reshape(bsz, nb, MOBA_BLOCK, D_WIDTH), vt


def _moba_kernel(qt_ref, k_ref, vt_ref, o_ref, kmean_ref, sel_ref, m_ref, acc_ref,
                 sa_ref, sb_ref):
    qi = pl.program_id(2)
    nb = k_ref.shape[0]
    width, tq = qt_ref.shape
    heads = width // HEAD_DIM
    neg_inf = -jnp.inf

    @pl.when(qi == 0)
    def _():
        for n in range(nb):
            kmean_ref[n:n + 1, :] = (jnp.sum(k_ref[n].astype(F32), axis=0, keepdims=True)
                                     * (1.0 / MOBA_BLOCK))

    qt = qt_ref[...]
    slab_head = lax.broadcasted_iota(jnp.int32, (LANES, 1), 0) // HEAD_DIM
    lane_head = lax.broadcasted_iota(jnp.int32, (1, width), 1) // HEAD_DIM
    blk = lax.broadcasted_iota(jnp.int32, (nb, tq), 0)
    kpos = lax.broadcasted_iota(jnp.int32, (MOBA_BLOCK, tq), 0)
    qpos = lax.broadcasted_iota(jnp.int32, (MOBA_BLOCK, tq), 1)
    causal = kpos <= qpos
    valid = blk < qi

    def lanes_of(hd):
        return slice(hd // 2 * LANES, (hd // 2 + 1) * LANES)

    qs = [jnp.where(slab_head == hd % 2, qt[lanes_of(hd)], jnp.zeros((LANES, tq), BF16))
          for hd in range(heads)]

    def qk_head(gi, s_ref, hd):
        for u in range(KV_GROUP):
            s_ref[hd, u] = _dot(k_ref[gi * KV_GROUP + u, :, lanes_of(hd)], qs[hd])

    def qk(gi, s_ref):
        for hd in range(heads):
            qk_head(gi, s_ref, hd)

    def weights(st, shift):
        return jnp.exp2((st - shift).astype(BF16))

    def softmax_pv_head(gi, s_ref, hd):
        m_run = m_ref[hd]
        picked = []
        m_new = m_run
        for u in range(KV_GROUP):
            pick = sel_ref[hd, pl.ds(gi * KV_GROUP + u, 1), :] > 0.5
            col_max = jnp.max(s_ref[hd, u], axis=0, keepdims=True)
            m_new = jnp.maximum(m_new, jnp.where(pick, col_max, neg_inf))
            picked.append(pick)
        acc = jnp.exp2(m_run - m_new) * acc_ref[hd]
        for u in range(KV_GROUP):
            p = weights(s_ref[hd, u], jnp.where(picked[u], m_new, jnp.inf))
            acc = acc + _dot(vt_ref[gi * KV_GROUP + u, hd], p)
        m_ref[hd] = m_new
        acc_ref[hd] = acc

    kmean = kmean_ref[...]
    km_heads = jnp.concatenate(
        [jnp.where(lane_head == hd, kmean, 0.0) for hd in range(heads)], axis=0)
    gates = jnp.dot(km_heads, qt.astype(F32), preferred_element_type=F32,
                    precision=lax.Precision.HIGHEST)
    own = [_dot(k_ref[qi, :, lanes_of(hd)], qs[hd]) for hd in range(heads)]
    qk(0, sa_ref)

    for hd in range(heads):
        gate = jnp.where(valid, gates[hd * nb:(hd + 1) * nb], neg_inf)
        beats = []
        for m in range(nb):
            gm = gate[m:m + 1, :]
            tie = jnp.where(blk > m, 1, 0)
            beats.append(jnp.where(gm > gate, 1, jnp.where(gm == gate, tie, 0)))
        while len(beats) > 1:
            beats = [a + b for a, b in zip(beats[0::2], beats[1::2])]
        sel_ref[hd] = jnp.where(valid, jnp.where(beats[0] < MOBA_TOPK, 1.0, 0.0), 0.0)

    for hd in range(heads):
        st = jnp.where(causal, own[hd], neg_inf)
        m0 = jnp.max(st, axis=0, keepdims=True)
        m_ref[hd] = m0
        acc_ref[hd] = _dot(vt_ref[qi, hd], weights(st, m0))

    n_groups = (qi + KV_GROUP - 1) // KV_GROUP

    def pair(j, carry):
        qk(2 * j + 1, sb_ref)
        refill = jnp.minimum(2 * j + 2, nb // KV_GROUP - 1)
        for hd in range(heads):
            softmax_pv_head(2 * j, sa_ref, hd)
            qk_head(refill, sa_ref, hd)
        for hd in range(heads):
            softmax_pv_head(2 * j + 1, sb_ref, hd)
        return carry

    lax.fori_loop(0, n_groups // 2, pair, 0)

    @pl.when(n_groups % 2 == 1)
    def _():
        for hd in range(heads):
            softmax_pv_head(n_groups - 1, sa_ref, hd)

    out_t = jnp.concatenate(
        [acc_ref[hd, 0:HEAD_DIM, :] * (1.0 / acc_ref[hd, HEAD_DIM:HEAD_DIM + 1, :])
         for hd in range(heads)], axis=0)
    o_ref[...] = out_t.T.astype(BF16)


def _moba_attention(qt, k, vt):
    bsz, nb, dw, blk = qt.shape
    heads = HEADS_PER_STEP
    width = heads * HEAD_DIM
    s = nb * blk
    return pl.pallas_call(
        _moba_kernel,
        grid=(bsz, dw // width, nb),
        in_specs=[
            pl.BlockSpec((None, None, width, blk), lambda b, p, i: (b, i, p, 0)),
            pl.BlockSpec((None, nb, blk, width), lambda b, p, i: (b, 0, 0, p)),
            pl.BlockSpec((None, nb, heads, VT_ROWS, blk), lambda b, p, i: (b, 0, p, 0, 0)),
        ],
        out_specs=pl.BlockSpec((None, blk, width), lambda b, p, i: (b, i, p)),
        out_shape=jax.ShapeDtypeStruct((bsz, s, dw), BF16),
        scratch_shapes=[
            pltpu.VMEM((nb, width), F32),
            pltpu.VMEM((heads, nb, blk), F32),
            pltpu.VMEM((heads, 1, blk), F32),
            pltpu.VMEM((heads, VT_ROWS, blk), F32),
            pltpu.VMEM((heads, KV_GROUP, blk, blk), F32),
            pltpu.VMEM((heads, KV_GROUP, blk, blk), F32),
        ],
        compiler_params=pltpu.CompilerParams(
            dimension_semantics=("arbitrary", "arbitrary", "arbitrary"),
            vmem_limit_bytes=VMEM_LIMIT),
        name="moba_attention",
    )(qt, k, vt)


def kernel(x, c, positions, ab_w_in, ab_vnorm_g, ab_vnorm_b, ab_spatial_w, ab_spatial_b,
           ab_pool_w, ab_pool_scale, ab_w_out, cd_w_in, cd_conv_w, cd_w_out,
           ffn_w_up, ffn_conv_w, ffn_w_down, ada_w, ada_b, norm_g):
    depth = ada_w.shape[0]
    mod = _ada_mod(c, ada_w, ada_b)
    for i in range(depth):
        j = i // 2
        if i % 2 == 0:
            x = _mixer_ab(x, mod[i], norm_g[i], ab_w_in[j], ab_vnorm_g[j], ab_vnorm_b[j],
                          ab_spatial_w[j], ab_spatial_b[j], ab_pool_w[j], ab_pool_scale[j],
                          ab_w_out[j])
            x = _ffn(x, mod[i], norm_g[i], ffn_w_up[i], ffn_conv_w[i], ffn_w_down[i])
        else:
            yc, qt, k, vt = _mixer_cd_in(x, mod[i], norm_g[i], positions, cd_w_in[j], cd_conv_w[j])
            yd = _moba_attention(qt, k, vt)
            x = _ffn(x, mod[i], norm_g[i], ffn_w_up[i], ffn_conv_w[i], ffn_w_down[i],
                     mixer_out=(yc, yd, cd_w_out[j]))
    return x
```

```python
import functools

import jax
import jax.numpy as jnp
from jax import lax
from jax.experimental import pallas as pl
from jax.experimental.pallas import tpu as pltpu

F32 = jnp.float32
BF16 = jnp.bfloat16

D_MODEL = 1024
CHUNK = 128
A_GROUPS = 8
A_WIDTH = 1024
POOL_WINDOWS = (2, 4, 8, 16)
B_GROUP_DIM = 128
B_WIDTH = len(POOL_WINDOWS) * B_GROUP_DIM
POOL_HALO = 16
C_WIDTH = 512
CONV_WIDTH = 3
CONV_HALO = 8
D_HEADS = 8
HEAD_DIM = 64
D_WIDTH = D_HEADS * HEAD_DIM
ROT_DIM = HEAD_DIM // 4
ROPE_THETA = 500000.0
MOBA_BLOCK = 256
MOBA_TOPK = 3
D_FF = 2816
EPS = 1e-6

LANES = 128
TOKEN_TILE = 512
FF_CHUNK = 256
KV_GROUP = 4
HEADS_PER_STEP = 4
VT_ROWS = HEAD_DIM + 16
QK_SCALE = HEAD_DIM ** -0.5 * 1.4426950408889634
VMEM_LIMIT = 56 * 1024 * 1024

SH1, SC1, GT1, SH2, SC2, GT2 = range(6)


def _dot(a, b):
    return jnp.dot(a, b, preferred_element_type=F32)


def _rms(x):
    return x * lax.rsqrt(jnp.mean(x * x, axis=-1, keepdims=True) + EPS)


def _gelu_tanh(x):
    inner = 0.7978845608028654 * (x + 0.044715 * (x * x * x))
    return x * (0.5 * (1.0 + jnp.tanh(inner)))


def _sigmoid(x):
    return 1.0 / (1.0 + jnp.exp(-x))


def _modulated_norm(x, gain, shift, scale):
    return _rms(x) * (gain * (1.0 + scale)) + shift


def _causal_conv3(halo_ref, cols, cur, w):
    rows = cur.shape[0]
    prev = halo_ref[:, cols]
    first = lax.broadcasted_iota(jnp.int32, (CONV_HALO, 1), 0)
    shifted = []
    for lag in (1, 2):
        rolled = pltpu.roll(cur, lag, 0)
        head = jnp.where(first < lag, pltpu.roll(prev, lag, 0), rolled[0:CONV_HALO])
        shifted.append(jnp.concatenate([head, rolled[CONV_HALO:]], axis=0))
    halo_ref[:, cols] = cur[rows - CONV_HALO:rows]
    return w[0:1] * shifted[1] + w[1:2] * shifted[0] + w[2:3] * cur


def _mod_kernel(c_ref, w_ref, b_ref, o_ref):
    c = c_ref[...]
    act = c * _sigmoid(c)
    o_ref[...] = jnp.dot(act, w_ref[...], preferred_element_type=F32,
                         precision=lax.Precision.HIGHEST) + b_ref[...]


def _ada_mod(c, ada_w, ada_b):
    depth, d, n = ada_w.shape
    bsz = c.shape[0]
    rows = 8
    c_pad = jnp.pad(c, ((0, rows - bsz), (0, 0)))
    tn = 1536
    out = pl.pallas_call(
        _mod_kernel,
        grid=(depth, n // tn),
        in_specs=[
            pl.BlockSpec((rows, d), lambda l, j: (0, 0)),
            pl.BlockSpec((None, d, tn), lambda l, j: (l, 0, j)),
            pl.BlockSpec((None, 1, tn), lambda l, j: (l, 0, j)),
        ],
        out_specs=pl.BlockSpec((None, rows, tn), lambda l, j: (l, 0, j)),
        out_shape=jax.ShapeDtypeStruct((depth, rows, n), F32),
        compiler_params=pltpu.CompilerParams(
            dimension_semantics=("arbitrary", "arbitrary"),
            vmem_limit_bytes=VMEM_LIMIT),
        name="ada_mod",
    )(c_pad, ada_w, ada_b.reshape(depth, 1, n))
    return out[:, :bsz].reshape(depth, bsz, 6, d)


def _mixer_ab_kernel(x_ref, mod_ref, g_ref, win_ref, vg_ref, vb_ref, ws_ref, bsf_ref,
                     pw_ref, ps_ref, wout_ref, o_ref, xb_ref, ycat_ref):
    t = pl.program_id(1)
    tm = x_ref.shape[0]

    @pl.when(t == 0)
    def _():
        xb_ref[0:POOL_HALO, :] = jnp.zeros((POOL_HALO, B_WIDTH), F32)

    x = x_ref[...]
    mod = mod_ref[...]
    g = g_ref[...]
    h = _modulated_norm(x, g[0:1], mod[SH1:SH1 + 1], mod[SC1:SC1 + 1]).astype(BF16)

    gv = _gelu_tanh(_dot(h, win_ref[:, A_WIDTH:2 * A_WIDTH]))
    mu = jnp.mean(gv, axis=-1, keepdims=True)
    dv = gv - mu
    var = jnp.mean(dv * dv, axis=-1, keepdims=True)
    vn = (dv * lax.rsqrt(var + EPS) * vg_ref[...] + vb_ref[...]).astype(BF16)
    gu = _gelu_tanh(_dot(h, win_ref[:, 0:A_WIDTH]))

    row = lax.broadcasted_iota(jnp.int32, (CHUNK, CHUNK), 0)
    col = lax.broadcasted_iota(jnp.int32, (CHUNK, CHUNK), 1)
    tril = row >= col
    for gi in range(A_GROUPS):
        cols = slice(gi * LANES, (gi + 1) * LANES)
        wg = jnp.where(tril, ws_ref[gi], 0.0).astype(BF16)
        bias = bsf_ref[:, cols]
        for ci in range(tm // CHUNK):
            rows = slice(ci * CHUNK, (ci + 1) * CHUNK)
            mixed = _dot(wg, vn[rows, cols]) + bias
            ycat_ref[rows, cols] = (gu[rows, cols] * mixed).astype(BF16)

    xb = _dot(h, win_ref[:, 2 * A_WIDTH:2 * A_WIDTH + B_WIDTH])
    xb_ref[POOL_HALO:POOL_HALO + tm, :] = xb
    pos = t * tm + lax.broadcasted_iota(jnp.int32, (tm, 1), 0)
    for gi, w in enumerate(POOL_WINDOWS):
        cols = slice(gi * LANES, (gi + 1) * LANES)
        s = xb[:, cols]
        for k in range(1, w):
            s = s + xb_ref[POOL_HALO - k:POOL_HALO - k + tm, cols]
        cnt = jnp.minimum(pos + 1, w).astype(F32)
        pooled = s * (1.0 / cnt) - xb[:, cols]
        mixed = _dot(pooled.astype(BF16), pw_ref[gi]) * ps_ref[:, cols]
        ycat_ref[:, A_WIDTH + gi * LANES:A_WIDTH + (gi + 1) * LANES] = mixed.astype(BF16)
    xb_ref[0:POOL_HALO, :] = xb_ref[tm:tm + POOL_HALO, :]

    y = _dot(ycat_ref[...], wout_ref[...])
    o_ref[...] = x + mod[GT1:GT1 + 1] * (_rms(y) * g[1:2])


def _const_spec(shape):
    zeros = (0,) * len(shape)
    return pl.BlockSpec(shape, lambda b, t: zeros, pipeline_mode=pl.Buffered(1))


def _mixer_ab(x, mod, g, w_in, vnorm_g, vnorm_b, w_s, b_s, pool_w, pool_scale, w_out):
    bsz, s, d = x.shape
    tm = TOKEN_TILE
    bias_full = jnp.repeat(b_s.T, LANES, axis=1)
    return pl.pallas_call(
        _mixer_ab_kernel,
        grid=(bsz, s // tm),
        in_specs=[
            pl.BlockSpec((None, tm, d), lambda b, t: (b, t, 0)),
            pl.BlockSpec((None, 6, d), lambda b, t: (b, 0, 0)),
            _const_spec((4, d)),
            _const_spec(w_in.shape),
            _const_spec((1, A_WIDTH)),
            _const_spec((1, A_WIDTH)),
            _const_spec(w_s.shape),
            _const_spec(bias_full.shape),
            _const_spec(pool_w.shape),
            _const_spec((1, B_WIDTH)),
            _const_spec(w_out.shape),
        ],
        out_specs=pl.BlockSpec((None, tm, d), lambda b, t: (b, t, 0)),
        out_shape=jax.ShapeDtypeStruct(x.shape, F32),
        scratch_shapes=[
            pltpu.VMEM((POOL_HALO + tm, B_WIDTH), F32),
            pltpu.VMEM((tm, A_WIDTH + B_WIDTH), BF16),
        ],
        compiler_params=pltpu.CompilerParams(
            dimension_semantics=("arbitrary", "arbitrary"),
            vmem_limit_bytes=VMEM_LIMIT),
        name="mixer_ab",
    )(x, mod, g, w_in.astype(BF16), vnorm_g.reshape(1, -1), vnorm_b.reshape(1, -1),
      w_s, bias_full, pool_w.astype(BF16), pool_scale.reshape(1, -1), w_out.astype(BF16))


def _ffn_kernel(fuse_mixer_out, *refs):
    if fuse_mixer_out:
        (x_ref, yc_ref, yd_ref, wo_ref, mod_ref, g_ref, wup_ref, cw_ref, wdn_ref,
         o_ref, halo_ref, act_ref) = refs
    else:
        (x_ref, mod_ref, g_ref, wup_ref, cw_ref, wdn_ref,
         o_ref, halo_ref, act_ref) = refs
    t = pl.program_id(1)

    @pl.when(t == 0)
    def _():
        halo_ref[...] = jnp.zeros(halo_ref.shape, F32)

    x = x_ref[...]
    mod = mod_ref[...]
    g = g_ref[...]
    if fuse_mixer_out:
        y = _dot(yc_ref[...], wo_ref[0:C_WIDTH, :]) + _dot(yd_ref[...], wo_ref[C_WIDTH:, :])
        x = x + mod[GT1:GT1 + 1] * (_rms(y) * g[1:2])
    h = _modulated_norm(x, g[2:3], mod[SH2:SH2 + 1], mod[SC2:SC2 + 1]).astype(BF16)

    for j in range(D_FF // FF_CHUNK):
        halves = []
        for half in range(2):
            start = half * D_FF + j * FF_CHUNK
            cols = slice(start, start + FF_CHUNK)
            up = _dot(h, wup_ref[:, cols])
            halves.append(_causal_conv3(halo_ref, cols, up, cw_ref[:, cols]))
        gate, lin = halves
        act = gate * _sigmoid(gate) * lin
        act_ref[:, j * FF_CHUNK:(j + 1) * FF_CHUNK] = act.astype(BF16)

    y = _dot(act_ref[...], wdn_ref[...])
    o_ref[...] = x + mod[GT2:GT2 + 1] * (_rms(y) * g[3:4])


def _layer_spec(stacked_shape, layer):
    zeros = (0,) * (len(stacked_shape) - 1)
    return pl.BlockSpec((None,) + tuple(stacked_shape[1:]), lambda b, t: (layer,) + zeros,
                        pipeline_mode=pl.Buffered(1))


def _ffn(x, mod, g, layer, w_up, conv_w, w_down, mixer_out=None):
    bsz, s, d = x.shape
    tm = TOKEN_TILE
    tile = lambda width: pl.BlockSpec((None, tm, width), lambda b, t: (b, t, 0))
    in_specs = [tile(d)]
    args = [x]
    if mixer_out is not None:
        yc, yd, w_o = mixer_out
        in_specs += [tile(C_WIDTH), tile(D_WIDTH), _const_spec(w_o.shape)]
        args += [yc, yd, w_o.astype(BF16)]
    in_specs += [
        pl.BlockSpec((None, 6, d), lambda b, t: (b, 0, 0)),
        _const_spec((4, d)),
        _layer_spec(w_up.shape, layer),
        _layer_spec(conv_w.shape, layer),
        _layer_spec(w_down.shape, layer),
    ]
    args += [mod, g, w_up, conv_w, w_down]
    return pl.pallas_call(
        functools.partial(_ffn_kernel, mixer_out is not None),
        grid=(bsz, s // tm),
        in_specs=in_specs,
        out_specs=tile(d),
        out_shape=jax.ShapeDtypeStruct(x.shape, F32),
        scratch_shapes=[
            pltpu.VMEM((CONV_HALO, 2 * D_FF), F32),
            pltpu.VMEM((tm, D_FF), BF16),
        ],
        compiler_params=pltpu.CompilerParams(
            dimension_semantics=("arbitrary", "arbitrary"),
            vmem_limit_bytes=VMEM_LIMIT),
        name="conv_ffn_fused" if mixer_out is not None else "conv_ffn",
    )(*args)


def _mixer_cd_in_kernel(x_ref, mod_ref, g_ref, pos_ref, invf_ref, win_ref, cw_ref,
                        yc_ref, qt_ref, k_ref, vt_ref, halo_ref):
    t = pl.program_id(1)
    tm = x_ref.shape[0]

    @pl.when(t == 0)
    def _():
        halo_ref[...] = jnp.zeros(halo_ref.shape, F32)

    mod = mod_ref[...]
    g = g_ref[...]
    h = _modulated_norm(x_ref[...], g[0:1], mod[SH1:SH1 + 1], mod[SC1:SC1 + 1]).astype(BF16)

    def proj(i):
        return _dot(h, win_ref[:, i * C_WIDTH:(i + 1) * C_WIDTH])

    prod = proj(1) * proj(2)
    conv = _causal_conv3(halo_ref, slice(0, C_WIDTH), prod, cw_ref[...])
    yc_ref[...] = (proj(0) * conv).astype(BF16)

    ang = pos_ref[...].astype(F32) * invf_ref[...]
    lane = lax.broadcasted_iota(jnp.int32, (1, LANES), 1) % HEAD_DIM
    half = ROT_DIM // 2
    cos = jnp.where(lane < ROT_DIM, jnp.cos(ang), 1.0)
    sin = jnp.sin(ang)
    sin = jnp.where(lane < half, -sin, jnp.where(lane < ROT_DIM, sin, 0.0))

    def rope(v):
        outs = []
        for ci in range(v.shape[1] // LANES):
            vc = v[:, ci * LANES:(ci + 1) * LANES]
            partner = jnp.where(lane < half, pltpu.roll(vc, LANES - half, 1),
                                pltpu.roll(vc, half, 1))
            outs.append(vc * cos + partner * sin)
        return jnp.concatenate(outs, axis=1)

    q = rope(proj(3)) * QK_SCALE
    k = rope(proj(4))
    v = proj(5)
    ones_rows = jnp.where(
        lax.broadcasted_iota(jnp.int32, (VT_ROWS - HEAD_DIM, MOBA_BLOCK), 0) == 0, 1.0, 0.0)
    for bi in range(tm // MOBA_BLOCK):
        rows = slice(bi * MOBA_BLOCK, (bi + 1) * MOBA_BLOCK)
        qt_ref[bi] = q[rows].T.astype(BF16)
        v_t = v[rows].T
        for hd in range(D_HEADS):
            vt_ref[bi, hd] = jnp.concatenate(
                [v_t[hd * HEAD_DIM:(hd + 1) * HEAD_DIM], ones_rows], axis=0).astype(BF16)
    k_ref[...] = k.astype(BF16)


def _mixer_cd_in(x, mod, g, positions, w_in, conv_w):
    bsz, s, d = x.shape
    tm = TOKEN_TILE
    nb = s // MOBA_BLOCK
    bpt = tm // MOBA_BLOCK
    lane = jnp.arange(LANES) % HEAD_DIM
    inv_freq = ROPE_THETA ** (-jnp.arange(0, ROT_DIM, 2, dtype=F32) / ROT_DIM)
    invf = jnp.where(lane < ROT_DIM, inv_freq[lane % (ROT_DIM // 2)], 0.0).reshape(1, LANES)
    yc, qt, k, vt = pl.pallas_call(
        _mixer_cd_in_kernel,
        grid=(bsz, s // tm),
        in_specs=[
            pl.BlockSpec((None, tm, d), lambda b, t: (b, t, 0)),
            pl.BlockSpec((None, 6, d), lambda b, t: (b, 0, 0)),
            _const_spec((4, d)),
            pl.BlockSpec((None, tm, 1), lambda b, t: (b, t, 0)),
            _const_spec((1, LANES)),
            _const_spec(w_in.shape),
            _const_spec(conv_w.shape),
        ],
        out_specs=[
            pl.BlockSpec((None, tm, C_WIDTH), lambda b, t: (b, t, 0)),
            pl.BlockSpec((None, bpt, D_WIDTH, MOBA_BLOCK), lambda b, t: (b, t, 0, 0)),
            pl.BlockSpec((None, tm, D_WIDTH), lambda b, t: (b, t, 0)),
            pl.BlockSpec((None, bpt, D_HEADS, VT_ROWS, MOBA_BLOCK), lambda b, t: (b, t, 0, 0, 0)),
        ],
        out_shape=[
            jax.ShapeDtypeStruct((bsz, s, C_WIDTH), BF16),
            jax.ShapeDtypeStruct((bsz, nb, D_WIDTH, MOBA_BLOCK), BF16),
            jax.ShapeDtypeStruct((bsz, s, D_WIDTH), BF16),
            jax.ShapeDtypeStruct((bsz, nb, D_HEADS, VT_ROWS, MOBA_BLOCK), BF16),
        ],
        scratch_shapes=[
            pltpu.VMEM((CONV_HALO, C_WIDTH), F32),
        ],
        compiler_params=pltpu.CompilerParams(
            dimension_semantics=("arbitrary", "arbitrary"),
            vmem_limit_bytes=VMEM_LIMIT),
        name="mixer_cd_in",
    )(x, mod, g, positions.reshape(bsz, s, 1), invf, w_in.astype(BF16), conv_w)
    return yc, qt, k.reshape(bsz, nb, MOBA_BLOCK, D_WIDTH), vt


def _moba_kernel(qt_ref, k_ref, vt_ref, o_ref, kmean_ref, sel_ref, m_ref, acc_ref,
                 sa_ref, sb_ref):
    qi = pl.program_id(2)
    nb = k_ref.shape[0]
    width, tq = qt_ref.shape
    heads = width // HEAD_DIM
    neg_inf = -jnp.inf

    @pl.when(qi == 0)
    def _():
        for n in range(nb):
            kmean_ref[n:n + 1, :] = (jnp.sum(k_ref[n].astype(F32), axis=0, keepdims=True)
                                     * (1.0 / MOBA_BLOCK))

    qt = qt_ref[...]
    slab_head = lax.broadcasted_iota(jnp.int32, (LANES, 1), 0) // HEAD_DIM
    lane_head = lax.broadcasted_iota(jnp.int32, (1, width), 1) // HEAD_DIM
    blk = lax.broadcasted_iota(jnp.int32, (nb, tq), 0)
    kpos = lax.broadcasted_iota(jnp.int32, (MOBA_BLOCK, tq), 0)
    qpos = lax.broadcasted_iota(jnp.int32, (MOBA_BLOCK, tq), 1)
    causal = kpos <= qpos
    valid = blk < qi

    def lanes_of(hd):
        return slice(hd // 2 * LANES, (hd // 2 + 1) * LANES)

    qs = [jnp.where(slab_head == hd % 2, qt[lanes_of(hd)], jnp.zeros((LANES, tq), BF16))
          for hd in range(heads)]

    def qk_head(gi, s_ref, hd):
        for u in range(KV_GROUP):
            s_ref[hd, u] = _dot(k_ref[gi * KV_GROUP + u, :, lanes_of(hd)], qs[hd])

    def qk(gi, s_ref):
        for hd in range(heads):
            qk_head(gi, s_ref, hd)

    def weights(st, shift):
        return jnp.exp2((st - shift).astype(BF16))

    def softmax_pv_head(gi, s_ref, hd):
        m_run = m_ref[hd]
        picked = []
        m_new = m_run
        for u in range(KV_GROUP):
            pick = sel_ref[hd, pl.ds(gi * KV_GROUP + u, 1), :] > 0.5
            col_max = jnp.max(s_ref[hd, u], axis=0, keepdims=True)
            m_new = jnp.maximum(m_new, jnp.where(pick, col_max, neg_inf))
            picked.append(pick)
        acc = jnp.exp2(m_run - m_new) * acc_ref[hd]
        for u in range(KV_GROUP):
            p = weights(s_ref[hd, u], jnp.where(picked[u], m_new, jnp.inf))
            acc = acc + _dot(vt_ref[gi * KV_GROUP + u, hd], p)
        m_ref[hd] = m_new
        acc_ref[hd] = acc

    kmean = kmean_ref[...]
    km_heads = jnp.concatenate(
        [jnp.where(lane_head == hd, kmean, 0.0) for hd in range(heads)], axis=0)
    gates = jnp.dot(km_heads, qt.astype(F32), preferred_element_type=F32,
                    precision=lax.Precision.HIGHEST)
    own = [_dot(k_ref[qi, :, lanes_of(hd)], qs[hd]) for hd in range(heads)]
    qk(0, sa_ref)

    for hd in range(heads):
        gate = jnp.where(valid, gates[hd * nb:(hd + 1) * nb], neg_inf)
        beats = []
        for m in range(nb):
            gm = gate[m:m + 1, :]
            tie = jnp.where(blk > m, 1, 0)
            beats.append(jnp.where(gm > gate, 1, jnp.where(gm == gate, tie, 0)))
        while len(beats) > 1:
            beats = [a + b for a, b in zip(beats[0::2], beats[1::2])]
        sel_ref[hd] = jnp.where(valid, jnp.where(beats[0] < MOBA_TOPK, 1.0, 0.0), 0.0)

    for hd in range(heads):
        st = jnp.where(causal, own[hd], neg_inf)
        m0 = jnp.max(st, axis=0, keepdims=True)
        m_ref[hd] = m0
        acc_ref[hd] = _dot(vt_ref[qi, hd], weights(st, m0))

    n_groups = (qi + KV_GROUP - 1) // KV_GROUP

    def pair(j, carry):
        qk(2 * j + 1, sb_ref)
        refill = jnp.minimum(2 * j + 2, nb // KV_GROUP - 1)
        for hd in range(heads):
            softmax_pv_head(2 * j, sa_ref, hd)
            qk_head(refill, sa_ref, hd)
        for hd in range(heads):
            softmax_pv_head(2 * j + 1, sb_ref, hd)
        return carry

    lax.fori_loop(0, n_groups // 2, pair, 0)

    @pl.when(n_groups % 2 == 1)
    def _():
        for hd in range(heads):
            softmax_pv_head(n_groups - 1, sa_ref, hd)

    out_t = jnp.concatenate(
        [acc_ref[hd, 0:HEAD_DIM, :] * (1.0 / acc_ref[hd, HEAD_DIM:HEAD_DIM + 1, :])
         for hd in range(heads)], axis=0)
    o_ref[...] = out_t.T.astype(BF16)


def _moba_attention(qt, k, vt):
    bsz, nb, dw, blk = qt.shape
    heads = HEADS_PER_STEP
    width = heads * HEAD_DIM
    s = nb * blk
    return pl.pallas_call(
        _moba_kernel,
        grid=(bsz, dw // width, nb),
        in_specs=[
            pl.BlockSpec((None, None, width, blk), lambda b, p, i: (b, i, p, 0)),
            pl.BlockSpec((None, nb, blk, width), lambda b, p, i: (b, 0, 0, p)),
            pl.BlockSpec((None, nb, heads, VT_ROWS, blk), lambda b, p, i: (b, 0, p, 0, 0)),
        ],
        out_specs=pl.BlockSpec((None, blk, width), lambda b, p, i: (b, i, p)),
        out_shape=jax.ShapeDtypeStruct((bsz, s, dw), BF16),
        scratch_shapes=[
            pltpu.VMEM((nb, width), F32),
            pltpu.VMEM((heads, nb, blk), F32),
            pltpu.VMEM((heads, 1, blk), F32),
            pltpu.VMEM((heads, VT_ROWS, blk), F32),
            pltpu.VMEM((heads, KV_GROUP, blk, blk), F32),
            pltpu.VMEM((heads, KV_GROUP, blk, blk), F32),
        ],
        compiler_params=pltpu.CompilerParams(
            dimension_semantics=("arbitrary", "arbitrary", "arbitrary"),
            vmem_limit_bytes=VMEM_LIMIT),
        name="moba_attention",
    )(qt, k, vt)


def kernel(x, c, positions, ab_w_in, ab_vnorm_g, ab_vnorm_b, ab_spatial_w, ab_spatial_b,
           ab_pool_w, ab_pool_scale, ab_w_out, cd_w_in, cd_conv_w, cd_w_out,
           ffn_w_up, ffn_conv_w, ffn_w_down, ada_w, ada_b, norm_g):
    depth = ada_w.shape[0]
    mod = _ada_mod(c, ada_w, ada_b)
    ffn = (ffn_w_up.astype(BF16), ffn_conv_w, ffn_w_down.astype(BF16))
    for i in range(depth):
        j = i // 2
        if i % 2 == 0:
            x = _mixer_ab(x, mod[i], norm_g[i], ab_w_in[j], ab_vnorm_g[j], ab_vnorm_b[j],
                          ab_spatial_w[j], ab_spatial_b[j], ab_pool_w[j], ab_pool_scale[j],
                          ab_w_out[j])
            x = _ffn(x, mod[i], norm_g[i], i, *ffn)
        else:
            yc, qt, k, vt = _mixer_cd_in(x, mod[i], norm_g[i], positions, cd_w_in[j], cd_conv_w[j])
            yd = _moba_attention(qt, k, vt)
            x = _ffn(x, mod[i], norm_g[i], i, *ffn, mixer_out=(yc, yd, cd_w_out[j]))
    return x
```

```python
import functools

import jax
import jax.numpy as jnp
from jax import lax
from jax.experimental import pallas as pl
from jax.experimental.pallas import tpu as pltpu

F32 = jnp.float32
BF16 = jnp.bfloat16

D_MODEL = 1024
CHUNK = 128
A_GROUPS = 8
A_WIDTH = 1024
POOL_WINDOWS = (2, 4, 8, 16)
B_GROUP_DIM = 128
B_WIDTH = len(POOL_WINDOWS) * B_GROUP_DIM
POOL_HALO = 16
C_WIDTH = 512
CONV_WIDTH = 3
CONV_HALO = 8
D_HEADS = 8
HEAD_DIM = 64
D_WIDTH = D_HEADS * HEAD_DIM
ROT_DIM = HEAD_DIM // 4
ROPE_THETA = 500000.0
MOBA_BLOCK = 256
MOBA_TOPK = 3
D_FF = 2816
EPS = 1e-6

LANES = 128
TOKEN_TILE = 512
FF_CHUNK = 256
KV_GROUP = 4
HEADS_PER_STEP = 8
VT_ROWS = HEAD_DIM + 16
QK_SCALE = HEAD_DIM ** -0.5 * 1.4426950408889634
VMEM_LIMIT = 56 * 1024 * 1024

SH1, SC1, GT1, SH2, SC2, GT2 = range(6)


def _dot(a, b):
    return jnp.dot(a, b, preferred_element_type=F32)


def _rms(x):
    return x * lax.rsqrt(jnp.mean(x * x, axis=-1, keepdims=True) + EPS)


def _gelu_tanh(x):
    inner = 0.7978845608028654 * (x + 0.044715 * (x * x * x))
    return x * (0.5 * (1.0 + jnp.tanh(inner)))


def _sigmoid(x):
    return 1.0 / (1.0 + jnp.exp(-x))


def _modulated_norm(x, gain, shift, scale):
    return _rms(x) * (gain * (1.0 + scale)) + shift


def _causal_conv3(halo_ref, cols, cur, w):
    rows = cur.shape[0]
    prev = halo_ref[:, cols]
    first = lax.broadcasted_iota(jnp.int32, (CONV_HALO, 1), 0)
    shifted = []
    for lag in (1, 2):
        rolled = pltpu.roll(cur, lag, 0)
        head = jnp.where(first < lag, pltpu.roll(prev, lag, 0), rolled[0:CONV_HALO])
        shifted.append(jnp.concatenate([head, rolled[CONV_HALO:]], axis=0))
    halo_ref[:, cols] = cur[rows - CONV_HALO:rows]
    return w[0:1] * shifted[1] + w[1:2] * shifted[0] + w[2:3] * cur


def _mod_kernel(c_ref, w_ref, b_ref, o_ref):
    c = c_ref[...]
    act = c * _sigmoid(c)
    o_ref[...] = jnp.dot(act, w_ref[...], preferred_element_type=F32,
                         precision=lax.Precision.HIGHEST) + b_ref[...]


def _ada_mod(c, ada_w, ada_b):
    depth, d, n = ada_w.shape
    bsz = c.shape[0]
    rows = 8
    c_pad = jnp.pad(c, ((0, rows - bsz), (0, 0)))
    tn = 1536
    out = pl.pallas_call(
        _mod_kernel,
        grid=(depth, n // tn),
        in_specs=[
            pl.BlockSpec((rows, d), lambda l, j: (0, 0)),
            pl.BlockSpec((None, d, tn), lambda l, j: (l, 0, j)),
            pl.BlockSpec((None, 1, tn), lambda l, j: (l, 0, j)),
        ],
        out_specs=pl.BlockSpec((None, rows, tn), lambda l, j: (l, 0, j)),
        out_shape=jax.ShapeDtypeStruct((depth, rows, n), F32),
        compiler_params=pltpu.CompilerParams(
            dimension_semantics=("arbitrary", "arbitrary"),
            vmem_limit_bytes=VMEM_LIMIT),
        name="ada_mod",
    )(c_pad, ada_w, ada_b.reshape(depth, 1, n))
    return out[:, :bsz].reshape(depth, bsz, 6, d)


def _mixer_ab_kernel(x_ref, mod_ref, g_ref, win_ref, vg_ref, vb_ref, ws_ref, bsf_ref,
                     pw_ref, ps_ref, wout_ref, o_ref, xb_ref, ycat_ref):
    t = pl.program_id(1)
    tm = x_ref.shape[0]

    @pl.when(t == 0)
    def _():
        xb_ref[0:POOL_HALO, :] = jnp.zeros((POOL_HALO, B_WIDTH), F32)

    x = x_ref[...]
    mod = mod_ref[...]
    g = g_ref[...]
    h = _modulated_norm(x, g[0:1], mod[SH1:SH1 + 1], mod[SC1:SC1 + 1]).astype(BF16)

    gv = _gelu_tanh(_dot(h, win_ref[:, A_WIDTH:2 * A_WIDTH]))
    mu = jnp.mean(gv, axis=-1, keepdims=True)
    dv = gv - mu
    var = jnp.mean(dv * dv, axis=-1, keepdims=True)
    vn = (dv * lax.rsqrt(var + EPS) * vg_ref[...] + vb_ref[...]).astype(BF16)
    gu = _gelu_tanh(_dot(h, win_ref[:, 0:A_WIDTH]))

    row = lax.broadcasted_iota(jnp.int32, (CHUNK, CHUNK), 0)
    col = lax.broadcasted_iota(jnp.int32, (CHUNK, CHUNK), 1)
    tril = row >= col
    for gi in range(A_GROUPS):
        cols = slice(gi * LANES, (gi + 1) * LANES)
        wg = jnp.where(tril, ws_ref[gi], 0.0).astype(BF16)
        bias = bsf_ref[:, cols]
        for ci in range(tm // CHUNK):
            rows = slice(ci * CHUNK, (ci + 1) * CHUNK)
            mixed = _dot(wg, vn[rows, cols]) + bias
            ycat_ref[rows, cols] = (gu[rows, cols] * mixed).astype(BF16)

    xb = _dot(h, win_ref[:, 2 * A_WIDTH:2 * A_WIDTH + B_WIDTH])
    xb_ref[POOL_HALO:POOL_HALO + tm, :] = xb
    pos = t * tm + lax.broadcasted_iota(jnp.int32, (tm, 1), 0)
    for gi, w in enumerate(POOL_WINDOWS):
        cols = slice(gi * LANES, (gi + 1) * LANES)
        s = xb[:, cols]
        for k in range(1, w):
            s = s + xb_ref[POOL_HALO - k:POOL_HALO - k + tm, cols]
        cnt = jnp.minimum(pos + 1, w).astype(F32)
        pooled = s * (1.0 / cnt) - xb[:, cols]
        mixed = _dot(pooled.astype(BF16), pw_ref[gi]) * ps_ref[:, cols]
        ycat_ref[:, A_WIDTH + gi * LANES:A_WIDTH + (gi + 1) * LANES] = mixed.astype(BF16)
    xb_ref[0:POOL_HALO, :] = xb_ref[tm:tm + POOL_HALO, :]

    y = _dot(ycat_ref[...], wout_ref[...])
    o_ref[...] = x + mod[GT1:GT1 + 1] * (_rms(y) * g[1:2])


def _const_spec(shape):
    zeros = (0,) * len(shape)
    return pl.BlockSpec(shape, lambda b, t: zeros, pipeline_mode=pl.Buffered(1))


def _mixer_ab(x, mod, g, w_in, vnorm_g, vnorm_b, w_s, b_s, pool_w, pool_scale, w_out):
    bsz, s, d = x.shape
    tm = TOKEN_TILE
    bias_full = jnp.repeat(b_s.T, LANES, axis=1)
    return pl.pallas_call(
        _mixer_ab_kernel,
        grid=(bsz, s // tm),
        in_specs=[
            pl.BlockSpec((None, tm, d), lambda b, t: (b, t, 0)),
            pl.BlockSpec((None, 6, d), lambda b, t: (b, 0, 0)),
            _const_spec((4, d)),
            _const_spec(w_in.shape),
            _const_spec((1, A_WIDTH)),
            _const_spec((1, A_WIDTH)),
            _const_spec(w_s.shape),
            _const_spec(bias_full.shape),
            _const_spec(pool_w.shape),
            _const_spec((1, B_WIDTH)),
            _const_spec(w_out.shape),
        ],
        out_specs=pl.BlockSpec((None, tm, d), lambda b, t: (b, t, 0)),
        out_shape=jax.ShapeDtypeStruct(x.shape, F32),
        scratch_shapes=[
            pltpu.VMEM((POOL_HALO + tm, B_WIDTH), F32),
            pltpu.VMEM((tm, A_WIDTH + B_WIDTH), BF16),
        ],
        compiler_params=pltpu.CompilerParams(
            dimension_semantics=("arbitrary", "arbitrary"),
            vmem_limit_bytes=VMEM_LIMIT),
        name="mixer_ab",
    )(x, mod, g, w_in.astype(BF16), vnorm_g.reshape(1, -1), vnorm_b.reshape(1, -1),
      w_s, bias_full, pool_w.astype(BF16), pool_scale.reshape(1, -1), w_out.astype(BF16))


def _ffn_kernel(fuse_mixer_out, *refs):
    if fuse_mixer_out:
        (x_ref, yc_ref, yd_ref, wo_ref, mod_ref, g_ref, wup_ref, cw_ref, wdn_ref,
         o_ref, halo_ref, act_ref) = refs
    else:
        (x_ref, mod_ref, g_ref, wup_ref, cw_ref, wdn_ref,
         o_ref, halo_ref, act_ref) = refs
    t = pl.program_id(1)

    @pl.when(t == 0)
    def _():
        halo_ref[...] = jnp.zeros(halo_ref.shape, F32)

    x = x_ref[...]
    mod = mod_ref[...]
    g = g_ref[...]
    if fuse_mixer_out:
        y = _dot(yc_ref[...], wo_ref[0:C_WIDTH, :]) + _dot(yd_ref[...], wo_ref[C_WIDTH:, :])
        x = x + mod[GT1:GT1 + 1] * (_rms(y) * g[1:2])
    h = _modulated_norm(x, g[2:3], mod[SH2:SH2 + 1], mod[SC2:SC2 + 1]).astype(BF16)

    for j in range(D_FF // FF_CHUNK):
        halves = []
        for half in range(2):
            start = half * D_FF + j * FF_CHUNK
            cols = slice(start, start + FF_CHUNK)
            up = _dot(h, wup_ref[:, cols])
            halves.append(_causal_conv3(halo_ref, cols, up, cw_ref[:, cols]))
        gate, lin = halves
        act = gate * _sigmoid(gate) * lin
        act_ref[:, j * FF_CHUNK:(j + 1) * FF_CHUNK] = act.astype(BF16)

    y = _dot(act_ref[...], wdn_ref[...])
    o_ref[...] = x + mod[GT2:GT2 + 1] * (_rms(y) * g[3:4])


def _layer_spec(stacked_shape, layer):
    zeros = (0,) * (len(stacked_shape) - 1)
    return pl.BlockSpec((None,) + tuple(stacked_shape[1:]), lambda b, t: (layer,) + zeros,
                        pipeline_mode=pl.Buffered(1))


def _ffn(x, mod, g, layer, w_up, conv_w, w_down, mixer_out=None):
    bsz, s, d = x.shape
    tm = TOKEN_TILE
    tile = lambda width: pl.BlockSpec((None, tm, width), lambda b, t: (b, t, 0))
    in_specs = [tile(d)]
    args = [x]
    if mixer_out is not None:
        yc, yd, w_o = mixer_out
        in_specs += [tile(C_WIDTH), tile(D_WIDTH), _const_spec(w_o.shape)]
        args += [yc, yd, w_o.astype(BF16)]
    in_specs += [
        pl.BlockSpec((None, 6, d), lambda b, t: (b, 0, 0)),
        _const_spec((4, d)),
        _layer_spec(w_up.shape, layer),
        _layer_spec(conv_w.shape, layer),
        _layer_spec(w_down.shape, layer),
    ]
    args += [mod, g, w_up, conv_w, w_down]
    return pl.pallas_call(
        functools.partial(_ffn_kernel, mixer_out is not None),
        grid=(bsz, s // tm),
        in_specs=in_specs,
        out_specs=tile(d),
        out_shape=jax.ShapeDtypeStruct(x.shape, F32),
        scratch_shapes=[
            pltpu.VMEM((CONV_HALO, 2 * D_FF), F32),
            pltpu.VMEM((tm, D_FF), BF16),
        ],
        compiler_params=pltpu.CompilerParams(
            dimension_semantics=("arbitrary", "arbitrary"),
            vmem_limit_bytes=VMEM_LIMIT),
        name="conv_ffn_fused" if mixer_out is not None else "conv_ffn",
    )(*args)


def _mixer_cd_in_kernel(x_ref, mod_ref, g_ref, pos_ref, invf_ref, win_ref, cw_ref,
                        yc_ref, qt_ref, k_ref, vt_ref, halo_ref):
    t = pl.program_id(1)
    tm = x_ref.shape[0]

    @pl.when(t == 0)
    def _():
        halo_ref[...] = jnp.zeros(halo_ref.shape, F32)

    mod = mod_ref[...]
    g = g_ref[...]
    h = _modulated_norm(x_ref[...], g[0:1], mod[SH1:SH1 + 1], mod[SC1:SC1 + 1]).astype(BF16)

    def proj(i):
        return _dot(h, win_ref[:, i * C_WIDTH:(i + 1) * C_WIDTH])

    prod = proj(1) * proj(2)
    conv = _causal_conv3(halo_ref, slice(0, C_WIDTH), prod, cw_ref[...])
    yc_ref[...] = (proj(0) * conv).astype(BF16)

    ang = pos_ref[...].astype(F32) * invf_ref[...]
    lane = lax.broadcasted_iota(jnp.int32, (1, LANES), 1) % HEAD_DIM
    half = ROT_DIM // 2
    cos = jnp.where(lane < ROT_DIM, jnp.cos(ang), 1.0)
    sin = jnp.sin(ang)
    sin = jnp.where(lane < half, -sin, jnp.where(lane < ROT_DIM, sin, 0.0))

    def rope(v):
        outs = []
        for ci in range(v.shape[1] // LANES):
            vc = v[:, ci * LANES:(ci + 1) * LANES]
            partner = jnp.where(lane < half, pltpu.roll(vc, LANES - half, 1),
                                pltpu.roll(vc, half, 1))
            outs.append(vc * cos + partner * sin)
        return jnp.concatenate(outs, axis=1)

    q = rope(proj(3)) * QK_SCALE
    k = rope(proj(4))
    v = proj(5)
    ones_rows = jnp.where(
        lax.broadcasted_iota(jnp.int32, (VT_ROWS - HEAD_DIM, MOBA_BLOCK), 0) == 0, 1.0, 0.0)
    for bi in range(tm // MOBA_BLOCK):
        rows = slice(bi * MOBA_BLOCK, (bi + 1) * MOBA_BLOCK)
        qt_ref[bi] = q[rows].T.astype(BF16)
        v_t = v[rows].T
        for hd in range(D_HEADS):
            vt_ref[bi, hd] = jnp.concatenate(
                [v_t[hd * HEAD_DIM:(hd + 1) * HEAD_DIM], ones_rows], axis=0).astype(BF16)
    k_ref[...] = k.astype(BF16)


def _mixer_cd_in(x, mod, g, positions, w_in, conv_w):
    bsz, s, d = x.shape
    tm = TOKEN_TILE
    nb = s // MOBA_BLOCK
    bpt = tm // MOBA_BLOCK
    lane = jnp.arange(LANES) % HEAD_DIM
    inv_freq = ROPE_THETA ** (-jnp.arange(0, ROT_DIM, 2, dtype=F32) / ROT_DIM)
    invf = jnp.where(lane < ROT_DIM, inv_freq[lane % (ROT_DIM // 2)], 0.0).reshape(1, LANES)
    yc, qt, k, vt = pl.pallas_call(
        _mixer_cd_in_kernel,
        grid=(bsz, s // tm),
        in_specs=[
            pl.BlockSpec((None, tm, d), lambda b, t: (b, t, 0)),
            pl.BlockSpec((None, 6, d), lambda b, t: (b, 0, 0)),
            _const_spec((4, d)),
            pl.BlockSpec((None, tm, 1), lambda b, t: (b, t, 0)),
            _const_spec((1, LANES)),
            _const_spec(w_in.shape),
            _const_spec(conv_w.shape),
        ],
        out_specs=[
            pl.BlockSpec((None, tm, C_WIDTH), lambda b, t: (b, t, 0)),
            pl.BlockSpec((None, bpt, D_WIDTH, MOBA_BLOCK), lambda b, t: (b, t, 0, 0)),
            pl.BlockSpec((None, tm, D_WIDTH), lambda b, t: (b, t, 0)),
            pl.BlockSpec((None, bpt, D_HEADS, VT_ROWS, MOBA_BLOCK), lambda b, t: (b, t, 0, 0, 0)),
        ],
        out_shape=[
            jax.ShapeDtypeStruct((bsz, s, C_WIDTH), BF16),
            jax.ShapeDtypeStruct((bsz, nb, D_WIDTH, MOBA_BLOCK), BF16),
            jax.ShapeDtypeStruct((bsz, s, D_WIDTH), BF16),
            jax.ShapeDtypeStruct((bsz, nb, D_HEADS, VT_ROWS, MOBA_BLOCK), BF16),
        ],
        scratch_shapes=[
            pltpu.VMEM((CONV_HALO, C_WIDTH), F32),
        ],
        compiler_params=pltpu.CompilerParams(
            dimension_semantics=("arbitrary", "arbitrary"),
            vmem_limit_bytes=VMEM_LIMIT),
        name="mixer_cd_in",
    )(x, mod, g, positions.reshape(bsz, s, 1), invf, w_in.astype(BF16), conv_w)
    return yc, qt, k.reshape(bsz, nb, MOBA_BLOCK, D_WIDTH), vt


def _moba_kernel(qt_ref, k_ref, vt_ref, o_ref, kmean_ref):
    qi = pl.program_id(2)
    nb = k_ref.shape[0]
    width, tq = qt_ref.shape
    heads = width // HEAD_DIM
    neg_inf = -jnp.inf

    @pl.when(qi == 0)
    def _():
        for n in range(nb):
            kmean_ref[n:n + 1, :] = (jnp.sum(k_ref[n].astype(F32), axis=0, keepdims=True)
                                     * (1.0 / MOBA_BLOCK))

    def lanes_of(hd):
        return slice(hd // 2 * LANES, (hd // 2 + 1) * LANES)

    def weights(st, shift):
        return jnp.exp2((st - shift).astype(BF16))

    def tile(n_groups):
        qt = qt_ref[...]
        slab_head = lax.broadcasted_iota(jnp.int32, (LANES, 1), 0) // HEAD_DIM
        lane_head = lax.broadcasted_iota(jnp.int32, (1, width), 1) // HEAD_DIM
        blk = lax.broadcasted_iota(jnp.int32, (nb, tq), 0)
        kpos = lax.broadcasted_iota(jnp.int32, (MOBA_BLOCK, tq), 0)
        qpos = lax.broadcasted_iota(jnp.int32, (MOBA_BLOCK, tq), 1)
        causal = kpos <= qpos
        valid = blk < qi

        qs = [jnp.where(slab_head == hd % 2, qt[lanes_of(hd)], jnp.zeros((LANES, tq), BF16))
              for hd in range(heads)]

        def scores(g):
            return [[_dot(k_ref[g * KV_GROUP + u, :, lanes_of(hd)], qs[hd])
                     for u in range(KV_GROUP)] for hd in range(heads)]

        kmean = kmean_ref[...]
        km_heads = jnp.concatenate(
            [jnp.where(lane_head == hd, kmean, 0.0) for hd in range(heads)], axis=0)
        gates = jnp.dot(km_heads, qt.astype(F32), preferred_element_type=F32,
                        precision=lax.Precision.HIGHEST)
        own = [_dot(k_ref[qi, :, lanes_of(hd)], qs[hd]) for hd in range(heads)]
        nxt = scores(0) if n_groups else None

        picks = []
        for hd in range(heads):
            gate = jnp.where(valid, gates[hd * nb:(hd + 1) * nb], neg_inf)
            beats = []
            for m in range(nb):
                gm = gate[m:m + 1, :]
                tie = jnp.where(blk > m, 1, 0)
                beats.append(jnp.where(gm > gate, 1, jnp.where(gm == gate, tie, 0)))
            while len(beats) > 1:
                beats = [a + b for a, b in zip(beats[0::2], beats[1::2])]
            picks.append(jnp.where(valid, jnp.where(beats[0] < MOBA_TOPK, 1.0, 0.0), 0.0))

        state = []
        for hd in range(heads):
            st = jnp.where(causal, own[hd], neg_inf)
            m0 = jnp.max(st, axis=0, keepdims=True)
            state.append((m0, _dot(vt_ref[qi, hd], weights(st, m0))))

        for g in range(n_groups):
            cur = nxt
            nxt = scores(g + 1) if g + 1 < n_groups else None
            for hd in range(heads):
                m_run, acc = state[hd]
                picked = []
                m_new = m_run
                for u in range(KV_GROUP):
                    n = g * KV_GROUP + u
                    pick = picks[hd][n:n + 1, :] > 0.5
                    col_max = jnp.max(cur[hd][u], axis=0, keepdims=True)
                    m_new = jnp.maximum(m_new, jnp.where(pick, col_max, neg_inf))
                    picked.append(pick)
                acc = jnp.exp2(m_run - m_new) * acc
                for u in range(KV_GROUP):
                    p = weights(cur[hd][u], jnp.where(picked[u], m_new, jnp.inf))
                    acc = acc + _dot(vt_ref[g * KV_GROUP + u, hd], p)
                state[hd] = (m_new, acc)

        out_t = jnp.concatenate(
            [acc[0:HEAD_DIM, :] * (1.0 / acc[HEAD_DIM:HEAD_DIM + 1, :]) for _, acc in state],
            axis=0)
        o_ref[...] = out_t.T.astype(BF16)

    n_groups = (qi + KV_GROUP - 1) // KV_GROUP
    for count in range(nb // KV_GROUP + 1):
        pl.when(n_groups == count)(functools.partial(tile, count))


def _moba_attention(qt, k, vt):
    bsz, nb, dw, blk = qt.shape
    heads = HEADS_PER_STEP
    width = heads * HEAD_DIM
    s = nb * blk
    return pl.pallas_call(
        _moba_kernel,
        grid=(bsz, dw // width, nb),
        in_specs=[
            pl.BlockSpec((None, None, width, blk), lambda b, p, i: (b, i, p, 0)),
            pl.BlockSpec((None, nb, blk, width), lambda b, p, i: (b, 0, 0, p)),
            pl.BlockSpec((None, nb, heads, VT_ROWS, blk), lambda b, p, i: (b, 0, p, 0, 0)),
        ],
        out_specs=pl.BlockSpec((None, blk, width), lambda b, p, i: (b, i, p)),
        out_shape=jax.ShapeDtypeStruct((bsz, s, dw), BF16),
        scratch_shapes=[pltpu.VMEM((nb, width), F32)],
        compiler_params=pltpu.CompilerParams(
            dimension_semantics=("arbitrary", "arbitrary", "arbitrary"),
            vmem_limit_bytes=VMEM_LIMIT),
        name="moba_attention",
    )(qt, k, vt)


def kernel(x, c, positions, ab_w_in, ab_vnorm_g, ab_vnorm_b, ab_spatial_w, ab_spatial_b,
           ab_pool_w, ab_pool_scale, ab_w_out, cd_w_in, cd_conv_w, cd_w_out,
           ffn_w_up, ffn_conv_w, ffn_w_down, ada_w, ada_b, norm_g):
    depth = ada_w.shape[0]
    mod = _ada_mod(c, ada_w, ada_b)
    ffn = (ffn_w_up.astype(BF16), ffn_conv_w, ffn_w_down.astype(BF16))
    for i in range(depth):
        j = i // 2
        if i % 2 == 0:
            x = _mixer_ab(x, mod[i], norm_g[i], ab_w_in[j], ab_vnorm_g[j], ab_vnorm_b[j],
                          ab_spatial_w[j], ab_spatial_b[j], ab_pool_w[j], ab_pool_scale[j],
                          ab_w_out[j])
            x = _ffn(x, mod[i], norm_g[i], i, *ffn)
        else:
            yc, qt, k, vt = _mixer_cd_in(x, mod[i], norm_g[i], positions, cd_w_in[j], cd_conv_w[j])
            yd = _moba_attention(qt, k, vt)
            x = _ffn(x, mod[i], norm_g[i], i, *ffn, mixer_out=(yc, yd, cd_w_out[j]))
    return x
```

```python
import functools

import jax
import jax.numpy as jnp
from jax import lax
from jax.experimental import pallas as pl
from jax.experimental.pallas import tpu as pltpu

F32 = jnp.float32
BF16 = jnp.bfloat16

D_MODEL = 1024
CHUNK = 128
A_GROUPS = 8
A_WIDTH = 1024
POOL_WINDOWS = (2, 4, 8, 16)
B_GROUP_DIM = 128
B_WIDTH = len(POOL_WINDOWS) * B_GROUP_DIM
POOL_HALO = 16
C_WIDTH = 512
CONV_WIDTH = 3
CONV_HALO = 8
D_HEADS = 8
HEAD_DIM = 64
D_WIDTH = D_HEADS * HEAD_DIM
ROT_DIM = HEAD_DIM // 4
ROPE_THETA = 500000.0
MOBA_BLOCK = 256
MOBA_TOPK = 3
D_FF = 2816
EPS = 1e-6

LANES = 128
TOKEN_TILE = 512
FF_CHUNK = 256
PROJ_CHUNK = 256
KV_GROUP = 4
KV_STEP = 2
HEADS_PER_STEP = 8
VT_ROWS = HEAD_DIM + 16
QK_SCALE = HEAD_DIM ** -0.5 * 1.4426950408889634
VMEM_LIMIT = 56 * 1024 * 1024

SH1, SC1, GT1, SH2, SC2, GT2 = range(6)


def _dot(a, b):
    return jnp.dot(a, b, preferred_element_type=F32)


def _rms(x):
    return x * lax.rsqrt(jnp.mean(x * x, axis=-1, keepdims=True) + EPS)


def _gelu_tanh(x):
    inner = 0.7978845608028654 * (x + 0.044715 * (x * x * x))
    return x * (0.5 * (1.0 + jnp.tanh(inner)))


def _sigmoid(x):
    return 1.0 / (1.0 + jnp.exp(-x))


def _modulated_norm(x, gain, shift, scale):
    return _rms(x) * (gain * (1.0 + scale)) + shift


def _causal_conv3(halo_ref, cols, cur, w):
    rows = cur.shape[0]
    prev = halo_ref[:, cols]
    first = lax.broadcasted_iota(jnp.int32, (CONV_HALO, 1), 0)
    shifted = []
    for lag in (1, 2):
        rolled = pltpu.roll(cur, lag, 0)
        head = jnp.where(first < lag, pltpu.roll(prev, lag, 0), rolled[0:CONV_HALO])
        shifted.append(jnp.concatenate([head, rolled[CONV_HALO:]], axis=0))
    halo_ref[:, cols] = cur[rows - CONV_HALO:rows]
    return w[0:1] * shifted[1] + w[1:2] * shifted[0] + w[2:3] * cur


def _mod_kernel(c_ref, w_ref, b_ref, o_ref):
    c = c_ref[...]
    act = c * _sigmoid(c)
    o_ref[...] = jnp.dot(act, w_ref[...], preferred_element_type=F32,
                         precision=lax.Precision.HIGHEST) + b_ref[...]


def _ada_mod(c, ada_w, ada_b):
    depth, d, n = ada_w.shape
    bsz = c.shape[0]
    rows = 8
    c_pad = jnp.pad(c, ((0, rows - bsz), (0, 0)))
    tn = 1536
    out = pl.pallas_call(
        _mod_kernel,
        grid=(depth, n // tn),
        in_specs=[
            pl.BlockSpec((rows, d), lambda l, j: (0, 0)),
            pl.BlockSpec((None, d, tn), lambda l, j: (l, 0, j)),
            pl.BlockSpec((None, 1, tn), lambda l, j: (l, 0, j)),
        ],
        out_specs=pl.BlockSpec((None, rows, tn), lambda l, j: (l, 0, j)),
        out_shape=jax.ShapeDtypeStruct((depth, rows, n), F32),
        compiler_params=pltpu.CompilerParams(
            dimension_semantics=("arbitrary", "arbitrary"),
            vmem_limit_bytes=VMEM_LIMIT),
        name="ada_mod",
    )(c_pad, ada_w, ada_b.reshape(depth, 1, n))
    return out[:, :bsz].reshape(depth, bsz, 6, d)


def _mixer_ab_kernel(x_ref, mod_ref, g_ref, win_ref, vg_ref, vb_ref, ws_ref, bsf_ref,
                     pw_ref, ps_ref, wout_ref, o_ref, xb_ref, ycat_ref):
    t = pl.program_id(1)
    tm = x_ref.shape[0]

    @pl.when(t == 0)
    def _():
        xb_ref[0:POOL_HALO, :] = jnp.zeros((POOL_HALO, B_WIDTH), F32)

    x = x_ref[...]
    mod = mod_ref[...]
    g = g_ref[...]
    h = _modulated_norm(x, g[0:1], mod[SH1:SH1 + 1], mod[SC1:SC1 + 1]).astype(BF16)

    def gelu_proj(start):
        return jnp.concatenate(
            [_gelu_tanh(_dot(h, win_ref[:, c:c + PROJ_CHUNK]))
             for c in range(start, start + A_WIDTH, PROJ_CHUNK)], axis=1)

    gv = gelu_proj(A_WIDTH)
    xb = _dot(h, win_ref[:, 2 * A_WIDTH:2 * A_WIDTH + B_WIDTH])
    mu = jnp.mean(gv, axis=-1, keepdims=True)
    dv = gv - mu
    var = jnp.mean(dv * dv, axis=-1, keepdims=True)
    vn = (dv * lax.rsqrt(var + EPS) * vg_ref[...] + vb_ref[...]).astype(BF16)
    gu = gelu_proj(0)

    row = lax.broadcasted_iota(jnp.int32, (CHUNK, CHUNK), 0)
    col = lax.broadcasted_iota(jnp.int32, (CHUNK, CHUNK), 1)
    tril = row >= col
    for gi in range(A_GROUPS):
        cols = slice(gi * LANES, (gi + 1) * LANES)
        wg = jnp.where(tril, ws_ref[gi], 0.0).astype(BF16)
        bias = bsf_ref[:, cols]
        for ci in range(tm // CHUNK):
            rows = slice(ci * CHUNK, (ci + 1) * CHUNK)
            mixed = _dot(wg, vn[rows, cols]) + bias
            ycat_ref[rows, cols] = (gu[rows, cols] * mixed).astype(BF16)

    xb_ref[POOL_HALO:POOL_HALO + tm, :] = xb
    pos = t * tm + lax.broadcasted_iota(jnp.int32, (tm, 1), 0)
    for gi, w in enumerate(POOL_WINDOWS):
        cols = slice(gi * LANES, (gi + 1) * LANES)
        s = xb[:, cols]
        for k in range(1, w):
            s = s + xb_ref[POOL_HALO - k:POOL_HALO - k + tm, cols]
        cnt = jnp.minimum(pos + 1, w).astype(F32)
        pooled = s * (1.0 / cnt) - xb[:, cols]
        mixed = _dot(pooled.astype(BF16), pw_ref[gi]) * ps_ref[:, cols]
        ycat_ref[:, A_WIDTH + gi * LANES:A_WIDTH + (gi + 1) * LANES] = mixed.astype(BF16)
    xb_ref[0:POOL_HALO, :] = xb_ref[tm:tm + POOL_HALO, :]

    y = _dot(ycat_ref[...], wout_ref[...])
    o_ref[...] = x + mod[GT1:GT1 + 1] * (_rms(y) * g[1:2])


def _const_spec(shape):
    zeros = (0,) * len(shape)
    return pl.BlockSpec(shape, lambda b, t: zeros, pipeline_mode=pl.Buffered(1))


def _mixer_ab(x, mod, g, w_in, vnorm_g, vnorm_b, w_s, b_s, pool_w, pool_scale, w_out):
    bsz, s, d = x.shape
    tm = TOKEN_TILE
    bias_full = jnp.repeat(b_s.T, LANES, axis=1)
    return pl.pallas_call(
        _mixer_ab_kernel,
        grid=(bsz, s // tm),
        in_specs=[
            pl.BlockSpec((None, tm, d), lambda b, t: (b, t, 0)),
            pl.BlockSpec((None, 6, d), lambda b, t: (b, 0, 0)),
            _const_spec((4, d)),
            _const_spec(w_in.shape),
            _const_spec((1, A_WIDTH)),
            _const_spec((1, A_WIDTH)),
            _const_spec(w_s.shape),
            _const_spec(bias_full.shape),
            _const_spec(pool_w.shape),
            _const_spec((1, B_WIDTH)),
            _const_spec(w_out.shape),
        ],
        out_specs=pl.BlockSpec((None, tm, d), lambda b, t: (b, t, 0)),
        out_shape=jax.ShapeDtypeStruct(x.shape, F32),
        scratch_shapes=[
            pltpu.VMEM((POOL_HALO + tm, B_WIDTH), F32),
            pltpu.VMEM((tm, A_WIDTH + B_WIDTH), BF16),
        ],
        compiler_params=pltpu.CompilerParams(
            dimension_semantics=("arbitrary", "arbitrary"),
            vmem_limit_bytes=VMEM_LIMIT),
        name="mixer_ab",
    )(x, mod, g, w_in.astype(BF16), vnorm_g.reshape(1, -1), vnorm_b.reshape(1, -1),
      w_s, bias_full, pool_w.astype(BF16), pool_scale.reshape(1, -1), w_out.astype(BF16))


def _ffn_kernel(fuse_mixer_out, *refs):
    if fuse_mixer_out:
        (x_ref, yc_ref, yd_ref, wo_ref, mod_ref, g_ref, wup_ref, cw_ref, wdn_ref,
         o_ref, halo_ref, act_ref) = refs
    else:
        (x_ref, mod_ref, g_ref, wup_ref, cw_ref, wdn_ref,
         o_ref, halo_ref, act_ref) = refs
    t = pl.program_id(1)

    @pl.when(t == 0)
    def _():
        halo_ref[...] = jnp.zeros(halo_ref.shape, F32)

    x = x_ref[...]
    mod = mod_ref[...]
    g = g_ref[...]
    if fuse_mixer_out:
        y = _dot(yc_ref[...], wo_ref[0:C_WIDTH, :]) + _dot(yd_ref[...], wo_ref[C_WIDTH:, :])
        x = x + mod[GT1:GT1 + 1] * (_rms(y) * g[1:2])
    h = _modulated_norm(x, g[2:3], mod[SH2:SH2 + 1], mod[SC2:SC2 + 1]).astype(BF16)

    for j in range(D_FF // FF_CHUNK):
        halves = []
        for half in range(2):
            start = half * D_FF + j * FF_CHUNK
            cols = slice(start, start + FF_CHUNK)
            up = _dot(h, wup_ref[:, cols])
            halves.append(_causal_conv3(halo_ref, cols, up, cw_ref[:, cols]))
        gate, lin = halves
        act = gate * _sigmoid(gate) * lin
        act_ref[:, j * FF_CHUNK:(j + 1) * FF_CHUNK] = act.astype(BF16)

    y = _dot(act_ref[...], wdn_ref[...])
    o_ref[...] = x + mod[GT2:GT2 + 1] * (_rms(y) * g[3:4])


def _layer_spec(stacked_shape, layer):
    zeros = (0,) * (len(stacked_shape) - 1)
    return pl.BlockSpec((None,) + tuple(stacked_shape[1:]), lambda b, t: (layer,) + zeros,
                        pipeline_mode=pl.Buffered(1))


def _ffn(x, mod, g, layer, w_up, conv_w, w_down, mixer_out=None):
    bsz, s, d = x.shape
    tm = TOKEN_TILE
    tile = lambda width: pl.BlockSpec((None, tm, width), lambda b, t: (b, t, 0))
    in_specs = [tile(d)]
    args = [x]
    if mixer_out is not None:
        yc, yd, w_o = mixer_out
        in_specs += [tile(C_WIDTH), tile(D_WIDTH), _const_spec(w_o.shape)]
        args += [yc, yd, w_o.astype(BF16)]
    in_specs += [
        pl.BlockSpec((None, 6, d), lambda b, t: (b, 0, 0)),
        _const_spec((4, d)),
        _layer_spec(w_up.shape, layer),
        _layer_spec(conv_w.shape, layer),
        _layer_spec(w_down.shape, layer),
    ]
    args += [mod, g, w_up, conv_w, w_down]
    return pl.pallas_call(
        functools.partial(_ffn_kernel, mixer_out is not None),
        grid=(bsz, s // tm),
        in_specs=in_specs,
        out_specs=tile(d),
        out_shape=jax.ShapeDtypeStruct(x.shape, F32),
        scratch_shapes=[
            pltpu.VMEM((CONV_HALO, 2 * D_FF), F32),
            pltpu.VMEM((tm, D_FF), BF16),
        ],
        compiler_params=pltpu.CompilerParams(
            dimension_semantics=("arbitrary", "arbitrary"),
            vmem_limit_bytes=VMEM_LIMIT),
        name="conv_ffn_fused" if mixer_out is not None else "conv_ffn",
    )(*args)


def _mixer_cd_in_kernel(x_ref, mod_ref, g_ref, pos_ref, invf_ref, win_ref, cw_ref,
                        yc_ref, qt_ref, k_ref, vt_ref, halo_ref):
    t = pl.program_id(1)
    tm = x_ref.shape[0]

    @pl.when(t == 0)
    def _():
        halo_ref[...] = jnp.zeros(halo_ref.shape, F32)

    mod = mod_ref[...]
    g = g_ref[...]
    h = _modulated_norm(x_ref[...], g[0:1], mod[SH1:SH1 + 1], mod[SC1:SC1 + 1]).astype(BF16)

    def proj(i):
        return _dot(h, win_ref[:, i * C_WIDTH:(i + 1) * C_WIDTH])

    prod = proj(1) * proj(2)
    conv = _causal_conv3(halo_ref, slice(0, C_WIDTH), prod, cw_ref[...])
    yc_ref[...] = (proj(0) * conv).astype(BF16)

    ang = pos_ref[...].astype(F32) * invf_ref[...]
    lane = lax.broadcasted_iota(jnp.int32, (1, LANES), 1) % HEAD_DIM
    half = ROT_DIM // 2
    cos = jnp.where(lane < ROT_DIM, jnp.cos(ang), 1.0)
    sin = jnp.sin(ang)
    sin = jnp.where(lane < half, -sin, jnp.where(lane < ROT_DIM, sin, 0.0))

    def rope(v):
        outs = []
        for ci in range(v.shape[1] // LANES):
            vc = v[:, ci * LANES:(ci + 1) * LANES]
            partner = jnp.where(lane < half, pltpu.roll(vc, LANES - half, 1),
                                pltpu.roll(vc, half, 1))
            outs.append(vc * cos + partner * sin)
        return jnp.concatenate(outs, axis=1)

    q = rope(proj(3)) * QK_SCALE
    k = rope(proj(4))
    v = proj(5)
    ones_rows = jnp.where(
        lax.broadcasted_iota(jnp.int32, (VT_ROWS - HEAD_DIM, MOBA_BLOCK), 0) == 0, 1.0, 0.0)
    for bi in range(tm // MOBA_BLOCK):
        rows = slice(bi * MOBA_BLOCK, (bi + 1) * MOBA_BLOCK)
        qt_ref[bi] = q[rows].T.astype(BF16)
        v_t = v[rows].T
        for hd in range(D_HEADS):
            vt_ref[bi, hd] = jnp.concatenate(
                [v_t[hd * HEAD_DIM:(hd + 1) * HEAD_DIM], ones_rows], axis=0).astype(BF16)
    k_ref[...] = k.astype(BF16)


def _mixer_cd_in(x, mod, g, positions, w_in, conv_w):
    bsz, s, d = x.shape
    tm = TOKEN_TILE
    nb = s // MOBA_BLOCK
    bpt = tm // MOBA_BLOCK
    lane = jnp.arange(LANES) % HEAD_DIM
    inv_freq = ROPE_THETA ** (-jnp.arange(0, ROT_DIM, 2, dtype=F32) / ROT_DIM)
    invf = jnp.where(lane < ROT_DIM, inv_freq[lane % (ROT_DIM // 2)], 0.0).reshape(1, LANES)
    yc, qt, k, vt = pl.pallas_call(
        _mixer_cd_in_kernel,
        grid=(bsz, s // tm),
        in_specs=[
            pl.BlockSpec((None, tm, d), lambda b, t: (b, t, 0)),
            pl.BlockSpec((None, 6, d), lambda b, t: (b, 0, 0)),
            _const_spec((4, d)),
            pl.BlockSpec((None, tm, 1), lambda b, t: (b, t, 0)),
            _const_spec((1, LANES)),
            _const_spec(w_in.shape),
            _const_spec(conv_w.shape),
        ],
        out_specs=[
            pl.BlockSpec((None, tm, C_WIDTH), lambda b, t: (b, t, 0)),
            pl.BlockSpec((None, bpt, D_WIDTH, MOBA_BLOCK), lambda b, t: (b, t, 0, 0)),
            pl.BlockSpec((None, tm, D_WIDTH), lambda b, t: (b, t, 0)),
            pl.BlockSpec((None, bpt, D_HEADS, VT_ROWS, MOBA_BLOCK), lambda b, t: (b, t, 0, 0, 0)),
        ],
        out_shape=[
            jax.ShapeDtypeStruct((bsz, s, C_WIDTH), BF16),
            jax.ShapeDtypeStruct((bsz, nb, D_WIDTH, MOBA_BLOCK), BF16),
            jax.ShapeDtypeStruct((bsz, s, D_WIDTH), BF16),
            jax.ShapeDtypeStruct((bsz, nb, D_HEADS, VT_ROWS, MOBA_BLOCK), BF16),
        ],
        scratch_shapes=[
            pltpu.VMEM((CONV_HALO, C_WIDTH), F32),
        ],
        compiler_params=pltpu.CompilerParams(
            dimension_semantics=("arbitrary", "arbitrary"),
            vmem_limit_bytes=VMEM_LIMIT),
        name="mixer_cd_in",
    )(x, mod, g, positions.reshape(bsz, s, 1), invf, w_in.astype(BF16), conv_w)
    return yc, qt, k.reshape(bsz, nb, MOBA_BLOCK, D_WIDTH), vt


def _moba_kernel(qt_ref, k_ref, vt_ref, o_ref, kmean_ref):
    qi = pl.program_id(2)
    nb = k_ref.shape[0]
    width, tq = qt_ref.shape
    heads = width // HEAD_DIM
    neg_inf = -jnp.inf

    @pl.when(qi == 0)
    def _():
        for n in range(nb):
            kmean_ref[n:n + 1, :] = (jnp.sum(k_ref[n].astype(F32), axis=0, keepdims=True)
                                     * (1.0 / MOBA_BLOCK))

    def lanes_of(hd):
        return slice(hd // 2 * LANES, (hd // 2 + 1) * LANES)

    def weights(st, shift):
        return jnp.exp2((st - shift).astype(BF16))

    def tile(n_past):
        groups = [(start, min(KV_GROUP, n_past - start)) for start in range(0, n_past, KV_GROUP)]
        qt = qt_ref[...]
        slab_head = lax.broadcasted_iota(jnp.int32, (LANES, 1), 0) // HEAD_DIM
        lane_head = lax.broadcasted_iota(jnp.int32, (1, width), 1) // HEAD_DIM
        blk = lax.broadcasted_iota(jnp.int32, (nb, tq), 0)
        kpos = lax.broadcasted_iota(jnp.int32, (MOBA_BLOCK, tq), 0)
        qpos = lax.broadcasted_iota(jnp.int32, (MOBA_BLOCK, tq), 1)
        causal = kpos <= qpos
        valid = blk < qi

        qs = [jnp.where(slab_head == hd % 2, qt[lanes_of(hd)], jnp.zeros((LANES, tq), BF16))
              for hd in range(heads)]

        def head_scores(group, hd):
            start, size = group
            return [_dot(k_ref[start + u, :, lanes_of(hd)], qs[hd]) for u in range(size)]

        kmean = kmean_ref[...]
        km_heads = jnp.concatenate(
            [jnp.where(lane_head == hd, kmean, 0.0) for hd in range(heads)], axis=0)
        gates = jnp.dot(km_heads, qt.astype(F32), preferred_element_type=F32,
                        precision=lax.Precision.HIGHEST)
        own = [_dot(k_ref[qi, :, lanes_of(hd)], qs[hd]) for hd in range(heads)]
        nxt = [head_scores(groups[0], hd) for hd in range(heads)] if groups else None

        picks = []
        for hd in range(heads):
            gate = jnp.where(valid, gates[hd * nb:(hd + 1) * nb], neg_inf)
            beats = []
            for m in range(nb):
                gm = gate[m:m + 1, :]
                tie = jnp.where(blk > m, 1, 0)
                beats.append(jnp.where(gm > gate, 1, jnp.where(gm == gate, tie, 0)))
            while len(beats) > 1:
                beats = [a + b for a, b in zip(beats[0::2], beats[1::2])]
            picks.append(jnp.where(valid, jnp.where(beats[0] < MOBA_TOPK, 1.0, 0.0), 0.0))

        state = []
        for hd in range(heads):
            st = jnp.where(causal, own[hd], neg_inf)
            m0 = jnp.max(st, axis=0, keepdims=True)
            state.append((m0, _dot(vt_ref[qi, hd], weights(st, m0))))

        for gi, (start, size) in enumerate(groups):
            cur = nxt
            nxt = []
            for hd in range(heads):
                if gi + 1 < len(groups):
                    nxt.append(head_scores(groups[gi + 1], hd))
                m_run, acc = state[hd]
                picked = []
                m_new = m_run
                for u in range(size):
                    pick = picks[hd][start + u:start + u + 1, :] > 0.5
                    col_max = jnp.max(cur[hd][u], axis=0, keepdims=True)
                    m_new = jnp.maximum(m_new, jnp.where(pick, col_max, neg_inf))
                    picked.append(pick)
                acc = jnp.exp2(m_run - m_new) * acc
                for u in range(size):
                    p = weights(cur[hd][u], jnp.where(picked[u], m_new, jnp.inf))
                    acc = acc + _dot(vt_ref[start + u, hd], p)
                state[hd] = (m_new, acc)

        out_t = jnp.concatenate(
            [acc[0:HEAD_DIM, :] * (1.0 / acc[HEAD_DIM:HEAD_DIM + 1, :]) for _, acc in state],
            axis=0)
        o_ref[...] = out_t.T.astype(BF16)

    steps = (qi + KV_STEP - 1) // KV_STEP
    for count in range(nb // KV_STEP + 1):
        pl.when(steps == count)(functools.partial(tile, count * KV_STEP))


def _moba_attention(qt, k, vt):
    bsz, nb, dw, blk = qt.shape
    heads = HEADS_PER_STEP
    width = heads * HEAD_DIM
    s = nb * blk
    return pl.pallas_call(
        _moba_kernel,
        grid=(bsz, dw // width, nb),
        in_specs=[
            pl.BlockSpec((None, None, width, blk), lambda b, p, i: (b, i, p, 0)),
            pl.BlockSpec((None, nb, blk, width), lambda b, p, i: (b, 0, 0, p)),
            pl.BlockSpec((None, nb, heads, VT_ROWS, blk), lambda b, p, i: (b, 0, p, 0, 0)),
        ],
        out_specs=pl.BlockSpec((None, blk, width), lambda b, p, i: (b, i, p)),
        out_shape=jax.ShapeDtypeStruct((bsz, s, dw), BF16),
        scratch_shapes=[pltpu.VMEM((nb, width), F32)],
        compiler_params=pltpu.CompilerParams(
            dimension_semantics=("arbitrary", "arbitrary", "arbitrary"),
            vmem_limit_bytes=VMEM_LIMIT),
        name="moba_attention",
    )(qt, k, vt)


def kernel(x, c, positions, ab_w_in, ab_vnorm_g, ab_vnorm_b, ab_spatial_w, ab_spatial_b,
           ab_pool_w, ab_pool_scale, ab_w_out, cd_w_in, cd_conv_w, cd_w_out,
           ffn_w_up, ffn_conv_w, ffn_w_down, ada_w, ada_b, norm_g):
    depth = ada_w.shape[0]
    mod = _ada_mod(c, ada_w, ada_b)
    ffn = (ffn_w_up.astype(BF16), ffn_conv_w, ffn_w_down.astype(BF16))
    for i in range(depth):
        j = i // 2
        if i % 2 == 0:
            x = _mixer_ab(x, mod[i], norm_g[i], ab_w_in[j], ab_vnorm_g[j], ab_vnorm_b[j],
                          ab_spatial_w[j], ab_spatial_b[j], ab_pool_w[j], ab_pool_scale[j],
                          ab_w_out[j])
            x = _ffn(x, mod[i], norm_g[i], i, *ffn)
        else:
            yc, qt, k, vt = _mixer_cd_in(x, mod[i], norm_g[i], positions, cd_w_in[j], cd_conv_w[j])
            yd = _moba_attention(qt, k, vt)
            x = _ffn(x, mod[i], norm_g[i], i, *ffn, mixer_out=(yc, yd, cd_w_out[j]))
    return x
```

```python
import functools

import jax
import jax.numpy as jnp
from jax import lax
from jax.experimental import pallas as pl
from jax.experimental.pallas import tpu as pltpu

F32 = jnp.float32
BF16 = jnp.bfloat16

D_MODEL = 1024
CHUNK = 128
A_GROUPS = 8
A_WIDTH = 1024
POOL_WINDOWS = (2, 4, 8, 16)
B_GROUP_DIM = 128
B_WIDTH = len(POOL_WINDOWS) * B_GROUP_DIM
POOL_HALO = 16
C_WIDTH = 512
CONV_WIDTH = 3
CONV_HALO = 8
D_HEADS = 8
HEAD_DIM = 64
D_WIDTH = D_HEADS * HEAD_DIM
ROT_DIM = HEAD_DIM // 4
ROPE_THETA = 500000.0
MOBA_BLOCK = 256
MOBA_TOPK = 3
D_FF = 2816
EPS = 1e-6

LANES = 128
TOKEN_TILE = 512
FF_CHUNK = 256
PROJ_CHUNK = 256
KV_GROUP = 4
KV_STEP = 2
HEADS_PER_STEP = 4
VT_ROWS = HEAD_DIM + 16
QK_SCALE = HEAD_DIM ** -0.5 * 1.4426950408889634
VMEM_LIMIT = 56 * 1024 * 1024

SH1, SC1, GT1, SH2, SC2, GT2 = range(6)


def _dot(a, b):
    return jnp.dot(a, b, preferred_element_type=F32)


def _rms(x):
    return x * lax.rsqrt(jnp.mean(x * x, axis=-1, keepdims=True) + EPS)


def _gelu_tanh(x):
    inner = 0.7978845608028654 * (x + 0.044715 * (x * x * x))
    return x * (0.5 * (1.0 + jnp.tanh(inner)))


def _sigmoid(x):
    return 1.0 / (1.0 + jnp.exp(-x))


def _modulated_norm(x, gain, shift, scale):
    return _rms(x) * (gain * (1.0 + scale)) + shift


def _causal_conv3(halo_ref, cols, cur, w):
    rows = cur.shape[0]
    prev = halo_ref[:, cols]
    first = lax.broadcasted_iota(jnp.int32, (CONV_HALO, 1), 0)
    shifted = []
    for lag in (1, 2):
        rolled = pltpu.roll(cur, lag, 0)
        head = jnp.where(first < lag, pltpu.roll(prev, lag, 0), rolled[0:CONV_HALO])
        shifted.append(jnp.concatenate([head, rolled[CONV_HALO:]], axis=0))
    halo_ref[:, cols] = cur[rows - CONV_HALO:rows]
    return w[0:1] * shifted[1] + w[1:2] * shifted[0] + w[2:3] * cur


def _mod_kernel(c_ref, w_ref, b_ref, o_ref):
    c = c_ref[...]
    act = c * _sigmoid(c)
    o_ref[...] = jnp.dot(act, w_ref[...], preferred_element_type=F32,
                         precision=lax.Precision.HIGHEST) + b_ref[...]


def _ada_mod(c, ada_w, ada_b):
    depth, d, n = ada_w.shape
    bsz = c.shape[0]
    rows = 8
    c_pad = jnp.pad(c, ((0, rows - bsz), (0, 0)))
    tn = 1536
    out = pl.pallas_call(
        _mod_kernel,
        grid=(depth, n // tn),
        in_specs=[
            pl.BlockSpec((rows, d), lambda l, j: (0, 0)),
            pl.BlockSpec((None, d, tn), lambda l, j: (l, 0, j)),
            pl.BlockSpec((None, 1, tn), lambda l, j: (l, 0, j)),
        ],
        out_specs=pl.BlockSpec((None, rows, tn), lambda l, j: (l, 0, j)),
        out_shape=jax.ShapeDtypeStruct((depth, rows, n), F32),
        compiler_params=pltpu.CompilerParams(
            dimension_semantics=("arbitrary", "arbitrary"),
            vmem_limit_bytes=VMEM_LIMIT),
        name="ada_mod",
    )(c_pad, ada_w, ada_b.reshape(depth, 1, n))
    return out[:, :bsz].reshape(depth, bsz, 6, d)


def _mixer_ab_kernel(x_ref, mod_ref, g_ref, win_ref, vg_ref, vb_ref, ws_ref, bsf_ref,
                     pw_ref, ps_ref, wout_ref, o_ref, xb_ref, ycat_ref):
    t = pl.program_id(1)
    tm = x_ref.shape[0]

    @pl.when(t == 0)
    def _():
        xb_ref[0:POOL_HALO, :] = jnp.zeros((POOL_HALO, B_WIDTH), F32)

    x = x_ref[...]
    mod = mod_ref[...]
    g = g_ref[...]
    h = _modulated_norm(x, g[0:1], mod[SH1:SH1 + 1], mod[SC1:SC1 + 1]).astype(BF16)

    def gelu_proj(start):
        return jnp.concatenate(
            [_gelu_tanh(_dot(h, win_ref[:, c:c + PROJ_CHUNK]))
             for c in range(start, start + A_WIDTH, PROJ_CHUNK)], axis=1)

    gv = gelu_proj(A_WIDTH)
    xb = _dot(h, win_ref[:, 2 * A_WIDTH:2 * A_WIDTH + B_WIDTH])
    mu = jnp.mean(gv, axis=-1, keepdims=True)
    dv = gv - mu
    var = jnp.mean(dv * dv, axis=-1, keepdims=True)
    vn = (dv * lax.rsqrt(var + EPS) * vg_ref[...] + vb_ref[...]).astype(BF16)
    gu = gelu_proj(0)

    row = lax.broadcasted_iota(jnp.int32, (CHUNK, CHUNK), 0)
    col = lax.broadcasted_iota(jnp.int32, (CHUNK, CHUNK), 1)
    tril = row >= col
    for gi in range(A_GROUPS):
        cols = slice(gi * LANES, (gi + 1) * LANES)
        wg = jnp.where(tril, ws_ref[gi], 0.0).astype(BF16)
        bias = bsf_ref[:, cols]
        for ci in range(tm // CHUNK):
            rows = slice(ci * CHUNK, (ci + 1) * CHUNK)
            mixed = _dot(wg, vn[rows, cols]) + bias
            ycat_ref[rows, cols] = (gu[rows, cols] * mixed).astype(BF16)

    xb_ref[POOL_HALO:POOL_HALO + tm, :] = xb
    pos = t * tm + lax.broadcasted_iota(jnp.int32, (tm, 1), 0)
    for gi, w in enumerate(POOL_WINDOWS):
        cols = slice(gi * LANES, (gi + 1) * LANES)
        s = xb[:, cols]
        for k in range(1, w):
            s = s + xb_ref[POOL_HALO - k:POOL_HALO - k + tm, cols]
        cnt = jnp.minimum(pos + 1, w).astype(F32)
        pooled = s * (1.0 / cnt) - xb[:, cols]
        mixed = _dot(pooled.astype(BF16), pw_ref[gi]) * ps_ref[:, cols]
        ycat_ref[:, A_WIDTH + gi * LANES:A_WIDTH + (gi + 1) * LANES] = mixed.astype(BF16)
    xb_ref[0:POOL_HALO, :] = xb_ref[tm:tm + POOL_HALO, :]

    y = _dot(ycat_ref[...], wout_ref[...])
    o_ref[...] = x + mod[GT1:GT1 + 1] * (_rms(y) * g[1:2])


def _const_spec(shape):
    zeros = (0,) * len(shape)
    return pl.BlockSpec(shape, lambda b, t: zeros, pipeline_mode=pl.Buffered(1))


def _mixer_ab(x, mod, g, w_in, vnorm_g, vnorm_b, w_s, b_s, pool_w, pool_scale, w_out):
    bsz, s, d = x.shape
    tm = TOKEN_TILE
    bias_full = jnp.repeat(b_s.T, LANES, axis=1)
    return pl.pallas_call(
        _mixer_ab_kernel,
        grid=(bsz, s // tm),
        in_specs=[
            pl.BlockSpec((None, tm, d), lambda b, t: (b, t, 0)),
            pl.BlockSpec((None, 6, d), lambda b, t: (b, 0, 0)),
            _const_spec((4, d)),
            _const_spec(w_in.shape),
            _const_spec((1, A_WIDTH)),
            _const_spec((1, A_WIDTH)),
            _const_spec(w_s.shape),
            _const_spec(bias_full.shape),
            _const_spec(pool_w.shape),
            _const_spec((1, B_WIDTH)),
            _const_spec(w_out.shape),
        ],
        out_specs=pl.BlockSpec((None, tm, d), lambda b, t: (b, t, 0)),
        out_shape=jax.ShapeDtypeStruct(x.shape, F32),
        scratch_shapes=[
            pltpu.VMEM((POOL_HALO + tm, B_WIDTH), F32),
            pltpu.VMEM((tm, A_WIDTH + B_WIDTH), BF16),
        ],
        compiler_params=pltpu.CompilerParams(
            dimension_semantics=("arbitrary", "arbitrary"),
            vmem_limit_bytes=VMEM_LIMIT),
        name="mixer_ab",
    )(x, mod, g, w_in.astype(BF16), vnorm_g.reshape(1, -1), vnorm_b.reshape(1, -1),
      w_s, bias_full, pool_w.astype(BF16), pool_scale.reshape(1, -1), w_out.astype(BF16))


def _ffn_kernel(fuse_mixer_out, *refs):
    if fuse_mixer_out:
        (x_ref, yc_ref, yd_ref, wo_ref, mod_ref, g_ref, wup_ref, cw_ref, wdn_ref,
         o_ref, halo_ref, act_ref) = refs
    else:
        (x_ref, mod_ref, g_ref, wup_ref, cw_ref, wdn_ref,
         o_ref, halo_ref, act_ref) = refs
    t = pl.program_id(1)

    @pl.when(t == 0)
    def _():
        halo_ref[...] = jnp.zeros(halo_ref.shape, F32)

    x = x_ref[...]
    mod = mod_ref[...]
    g = g_ref[...]
    if fuse_mixer_out:
        y = _dot(yc_ref[...], wo_ref[0:C_WIDTH, :]) + _dot(yd_ref[...], wo_ref[C_WIDTH:, :])
        x = x + mod[GT1:GT1 + 1] * (_rms(y) * g[1:2])
    h = _modulated_norm(x, g[2:3], mod[SH2:SH2 + 1], mod[SC2:SC2 + 1]).astype(BF16)

    for j in range(D_FF // FF_CHUNK):
        halves = []
        for half in range(2):
            start = half * D_FF + j * FF_CHUNK
            cols = slice(start, start + FF_CHUNK)
            up = _dot(h, wup_ref[:, cols])
            halves.append(_causal_conv3(halo_ref, cols, up, cw_ref[:, cols]))
        gate, lin = halves
        act = gate * _sigmoid(gate) * lin
        act_ref[:, j * FF_CHUNK:(j + 1) * FF_CHUNK] = act.astype(BF16)

    y = _dot(act_ref[...], wdn_ref[...])
    o_ref[...] = x + mod[GT2:GT2 + 1] * (_rms(y) * g[3:4])


def _layer_spec(stacked_shape, layer):
    zeros = (0,) * (len(stacked_shape) - 1)
    return pl.BlockSpec((None,) + tuple(stacked_shape[1:]), lambda b, t: (layer,) + zeros,
                        pipeline_mode=pl.Buffered(1))


def _ffn(x, mod, g, layer, w_up, conv_w, w_down, mixer_out=None):
    bsz, s, d = x.shape
    tm = TOKEN_TILE
    tile = lambda width: pl.BlockSpec((None, tm, width), lambda b, t: (b, t, 0))
    in_specs = [tile(d)]
    args = [x]
    if mixer_out is not None:
        yc, yd, w_o = mixer_out
        in_specs += [tile(C_WIDTH), tile(D_WIDTH), _const_spec(w_o.shape)]
        args += [yc, yd, w_o.astype(BF16)]
    in_specs += [
        pl.BlockSpec((None, 6, d), lambda b, t: (b, 0, 0)),
        _const_spec((4, d)),
        _layer_spec(w_up.shape, layer),
        _layer_spec(conv_w.shape, layer),
        _layer_spec(w_down.shape, layer),
    ]
    args += [mod, g, w_up, conv_w, w_down]
    return pl.pallas_call(
        functools.partial(_ffn_kernel, mixer_out is not None),
        grid=(bsz, s // tm),
        in_specs=in_specs,
        out_specs=tile(d),
        out_shape=jax.ShapeDtypeStruct(x.shape, F32),
        scratch_shapes=[
            pltpu.VMEM((CONV_HALO, 2 * D_FF), F32),
            pltpu.VMEM((tm, D_FF), BF16),
        ],
        compiler_params=pltpu.CompilerParams(
            dimension_semantics=("arbitrary", "arbitrary"),
            vmem_limit_bytes=VMEM_LIMIT),
        name="conv_ffn_fused" if mixer_out is not None else "conv_ffn",
    )(*args)


def _mixer_cd_in_kernel(x_ref, mod_ref, g_ref, pos_ref, invf_ref, win_ref, cw_ref,
                        yc_ref, qt_ref, k_ref, vt_ref, halo_ref):
    t = pl.program_id(1)
    tm = x_ref.shape[0]

    @pl.when(t == 0)
    def _():
        halo_ref[...] = jnp.zeros(halo_ref.shape, F32)

    mod = mod_ref[...]
    g = g_ref[...]
    h = _modulated_norm(x_ref[...], g[0:1], mod[SH1:SH1 + 1], mod[SC1:SC1 + 1]).astype(BF16)

    def proj(i):
        return _dot(h, win_ref[:, i * C_WIDTH:(i + 1) * C_WIDTH])

    prod = proj(1) * proj(2)
    conv = _causal_conv3(halo_ref, slice(0, C_WIDTH), prod, cw_ref[...])
    yc_ref[...] = (proj(0) * conv).astype(BF16)

    ang = pos_ref[...].astype(F32) * invf_ref[...]
    lane = lax.broadcasted_iota(jnp.int32, (1, LANES), 1) % HEAD_DIM
    half = ROT_DIM // 2
    cos = jnp.where(lane < ROT_DIM, jnp.cos(ang), 1.0)
    sin = jnp.sin(ang)
    sin = jnp.where(lane < half, -sin, jnp.where(lane < ROT_DIM, sin, 0.0))

    def rope(v):
        outs = []
        for ci in range(v.shape[1] // LANES):
            vc = v[:, ci * LANES:(ci + 1) * LANES]
            partner = jnp.where(lane < half, pltpu.roll(vc, LANES - half, 1),
                                pltpu.roll(vc, half, 1))
            outs.append(vc * cos + partner * sin)
        return jnp.concatenate(outs, axis=1)

    q = rope(proj(3)) * QK_SCALE
    k = rope(proj(4))
    v = proj(5)
    ones_rows = jnp.where(
        lax.broadcasted_iota(jnp.int32, (VT_ROWS - HEAD_DIM, MOBA_BLOCK), 0) == 0, 1.0, 0.0)
    for bi in range(tm // MOBA_BLOCK):
        rows = slice(bi * MOBA_BLOCK, (bi + 1) * MOBA_BLOCK)
        qt_ref[bi] = q[rows].T.astype(BF16)
        v_t = v[rows].T
        for hd in range(D_HEADS):
            vt_ref[bi, hd] = jnp.concatenate(
                [v_t[hd * HEAD_DIM:(hd + 1) * HEAD_DIM], ones_rows], axis=0).astype(BF16)
    k_ref[...] = k.astype(BF16)


def _mixer_cd_in(x, mod, g, positions, w_in, conv_w):
    bsz, s, d = x.shape
    tm = TOKEN_TILE
    nb = s // MOBA_BLOCK
    bpt = tm // MOBA_BLOCK
    lane = jnp.arange(LANES) % HEAD_DIM
    inv_freq = ROPE_THETA ** (-jnp.arange(0, ROT_DIM, 2, dtype=F32) / ROT_DIM)
    invf = jnp.where(lane < ROT_DIM, inv_freq[lane % (ROT_DIM // 2)], 0.0).reshape(1, LANES)
    yc, qt, k, vt = pl.pallas_call(
        _mixer_cd_in_kernel,
        grid=(bsz, s // tm),
        in_specs=[
            pl.BlockSpec((None, tm, d), lambda b, t: (b, t, 0)),
            pl.BlockSpec((None, 6, d), lambda b, t: (b, 0, 0)),
            _const_spec((4, d)),
            pl.BlockSpec((None, tm, 1), lambda b, t: (b, t, 0)),
            _const_spec((1, LANES)),
            _const_spec(w_in.shape),
            _const_spec(conv_w.shape),
        ],
        out_specs=[
            pl.BlockSpec((None, tm, C_WIDTH), lambda b, t: (b, t, 0)),
            pl.BlockSpec((None, bpt, D_WIDTH, MOBA_BLOCK), lambda b, t: (b, t, 0, 0)),
            pl.BlockSpec((None, tm, D_WIDTH), lambda b, t: (b, t, 0)),
            pl.BlockSpec((None, bpt, D_HEADS, VT_ROWS, MOBA_BLOCK), lambda b, t: (b, t, 0, 0, 0)),
        ],
        out_shape=[
            jax.ShapeDtypeStruct((bsz, s, C_WIDTH), BF16),
            jax.ShapeDtypeStruct((bsz, nb, D_WIDTH, MOBA_BLOCK), BF16),
            jax.ShapeDtypeStruct((bsz, s, D_WIDTH), BF16),
            jax.ShapeDtypeStruct((bsz, nb, D_HEADS, VT_ROWS, MOBA_BLOCK), BF16),
        ],
        scratch_shapes=[
            pltpu.VMEM((CONV_HALO, C_WIDTH), F32),
        ],
        compiler_params=pltpu.CompilerParams(
            dimension_semantics=("arbitrary", "arbitrary"),
            vmem_limit_bytes=VMEM_LIMIT),
        name="mixer_cd_in",
    )(x, mod, g, positions.reshape(bsz, s, 1), invf, w_in.astype(BF16), conv_w)
    return yc, qt, k.reshape(bsz, nb, MOBA_BLOCK, D_WIDTH), vt


def _moba_kernel(qt_ref, k_ref, vt_ref, o_ref, kmean_ref):
    qi = pl.program_id(2)
    nb = k_ref.shape[0]
    width, tq = qt_ref.shape
    heads = width // HEAD_DIM
    neg_inf = -jnp.inf

    @pl.when(qi == 0)
    def _():
        for n in range(nb):
            kmean_ref[n:n + 1, :] = (jnp.sum(k_ref[n].astype(F32), axis=0, keepdims=True)
                                     * (1.0 / MOBA_BLOCK))

    def lanes_of(hd):
        return slice(hd // 2 * LANES, (hd // 2 + 1) * LANES)

    def weights(st, shift):
        return jnp.exp2((st - shift).astype(BF16))

    def tile(n_past):
        groups = [(start, min(KV_GROUP, n_past - start)) for start in range(0, n_past, KV_GROUP)]
        qt = qt_ref[...]
        slab_head = lax.broadcasted_iota(jnp.int32, (LANES, 1), 0) // HEAD_DIM
        lane_head = lax.broadcasted_iota(jnp.int32, (1, width), 1) // HEAD_DIM
        blk = lax.broadcasted_iota(jnp.int32, (nb, tq), 0)
        kpos = lax.broadcasted_iota(jnp.int32, (MOBA_BLOCK, tq), 0)
        qpos = lax.broadcasted_iota(jnp.int32, (MOBA_BLOCK, tq), 1)
        causal = kpos <= qpos
        valid = blk < qi

        qs = [jnp.where(slab_head == hd % 2, qt[lanes_of(hd)], jnp.zeros((LANES, tq), BF16))
              for hd in range(heads)]

        def head_scores(group, hd):
            start, size = group
            return [_dot(k_ref[start + u, :, lanes_of(hd)], qs[hd]) for u in range(size)]

        kmean = kmean_ref[...]
        km_heads = jnp.concatenate(
            [jnp.where(lane_head == hd, kmean, 0.0) for hd in range(heads)], axis=0)
        gates = jnp.dot(km_heads, qt.astype(F32), preferred_element_type=F32,
                        precision=lax.Precision.HIGHEST)
        own = [_dot(k_ref[qi, :, lanes_of(hd)], qs[hd]) for hd in range(heads)]
        nxt = [head_scores(groups[0], hd) for hd in range(heads)] if groups else None

        picks = []
        for hd in range(heads):
            gate = jnp.where(valid, gates[hd * nb:(hd + 1) * nb], neg_inf)
            beats = []
            for m in range(nb):
                gm = gate[m:m + 1, :]
                tie = jnp.where(blk > m, 1, 0)
                beats.append(jnp.where(gm > gate, 1, jnp.where(gm == gate, tie, 0)))
            while len(beats) > 1:
                beats = [a + b for a, b in zip(beats[0::2], beats[1::2])]
            picks.append(jnp.where(valid, jnp.where(beats[0] < MOBA_TOPK, 1.0, 0.0), 0.0))

        state = []
        for hd in range(heads):
            st = jnp.where(causal, own[hd], neg_inf)
            m0 = jnp.max(st, axis=0, keepdims=True)
            state.append((m0, _dot(vt_ref[qi, hd], weights(st, m0))))

        for gi, (start, size) in enumerate(groups):
            cur = nxt
            nxt = []
            for hd in range(heads):
                if gi + 1 < len(groups):
                    nxt.append(head_scores(groups[gi + 1], hd))
                m_run, acc = state[hd]
                picked = []
                m_new = m_run
                for u in range(size):
                    pick = picks[hd][start + u:start + u + 1, :] > 0.5
                    col_max = jnp.max(cur[hd][u], axis=0, keepdims=True)
                    m_new = jnp.maximum(m_new, jnp.where(pick, col_max, neg_inf))
                    picked.append(pick)
                acc = jnp.exp2(m_run - m_new) * acc
                for u in range(size):
                    p = weights(cur[hd][u], jnp.where(picked[u], m_new, jnp.inf))
                    acc = acc + _dot(vt_ref[start + u, hd], p)
                state[hd] = (m_new, acc)

        out_t = jnp.concatenate(
            [acc[0:HEAD_DIM, :] * (1.0 / acc[HEAD_DIM:HEAD_DIM + 1, :]) for _, acc in state],
            axis=0)
        o_ref[...] = out_t.T.astype(BF16)

    steps = (qi + KV_STEP - 1) // KV_STEP
    for count in range(nb // KV_STEP + 1):
        pl.when(steps == count)(functools.partial(tile, count * KV_STEP))


def _moba_attention(qt, k, vt):
    bsz, nb, dw, blk = qt.shape
    heads = HEADS_PER_STEP
    width = heads * HEAD_DIM
    s = nb * blk
    return pl.pallas_call(
        _moba_kernel,
        grid=(bsz, dw // width, nb),
        in_specs=[
            pl.BlockSpec((None, None, width, blk), lambda b, p, i: (b, i, p, 0)),
            pl.BlockSpec((None, nb, blk, width), lambda b, p, i: (b, 0, 0, p)),
            pl.BlockSpec((None, nb, heads, VT_ROWS, blk), lambda b, p, i: (b, 0, p, 0, 0)),
        ],
        out_specs=pl.BlockSpec((None, blk, width), lambda b, p, i: (b, i, p)),
        out_shape=jax.ShapeDtypeStruct((bsz, s, dw), BF16),
        scratch_shapes=[pltpu.VMEM((nb, width), F32)],
        compiler_params=pltpu.CompilerParams(
            dimension_semantics=("arbitrary", "arbitrary", "arbitrary"),
            vmem_limit_bytes=VMEM_LIMIT),
        name="moba_attention",
    )(qt, k, vt)


def kernel(x, c, positions, ab_w_in, ab_vnorm_g, ab_vnorm_b, ab_spatial_w, ab_spatial_b,
           ab_pool_w, ab_pool_scale, ab_w_out, cd_w_in, cd_conv_w, cd_w_out,
           ffn_w_up, ffn_conv_w, ffn_w_down, ada_w, ada_b, norm_g):
    depth = ada_w.shape[0]
    mod = _ada_mod(c, ada_w, ada_b)
    ffn = (ffn_w_up.astype(BF16), ffn_conv_w, ffn_w_down.astype(BF16))
    for i in range(depth):
        j = i // 2
        if i % 2 == 0:
            x = _mixer_ab(x, mod[i], norm_g[i], ab_w_in[j], ab_vnorm_g[j], ab_vnorm_b[j],
                          ab_spatial_w[j], ab_spatial_b[j], ab_pool_w[j], ab_pool_scale[j],
                          ab_w_out[j])
            x = _ffn(x, mod[i], norm_g[i], i, *ffn)
        else:
            yc, qt, k, vt = _mixer_cd_in(x, mod[i], norm_g[i], positions, cd_w_in[j], cd_conv_w[j])
            yd = _moba_attention(qt, k, vt)
            x = _ffn(x, mod[i], norm_g[i], i, *ffn, mixer_out=(yc, yd, cd_w_out[j]))
    return x
```

```python
import functools

import jax
import jax.numpy as jnp
from jax import lax
from jax.experimental import pallas as pl
from jax.experimental.pallas import tpu as pltpu

F32 = jnp.float32
BF16 = jnp.bfloat16

D_MODEL = 1024
CHUNK = 128
A_GROUPS = 8
A_WIDTH = 1024
POOL_WINDOWS = (2, 4, 8, 16)
B_GROUP_DIM = 128
B_WIDTH = len(POOL_WINDOWS) * B_GROUP_DIM
POOL_HALO = 16
C_WIDTH = 512
CONV_WIDTH = 3
CONV_HALO = 8
D_HEADS = 8
HEAD_DIM = 64
D_WIDTH = D_HEADS * HEAD_DIM
ROT_DIM = HEAD_DIM // 4
ROPE_THETA = 500000.0
MOBA_BLOCK = 256
MOBA_TOPK = 3
D_FF = 2816
EPS = 1e-6

LANES = 128
TOKEN_TILE = 1024
FF_CHUNK = 256
PROJ_CHUNK = 256
KV_GROUP = 4
KV_STEP = 2
HEADS_PER_STEP = 4
VT_ROWS = HEAD_DIM + 16
QK_SCALE = HEAD_DIM ** -0.5 * 1.4426950408889634
VMEM_LIMIT = 56 * 1024 * 1024

SH1, SC1, GT1, SH2, SC2, GT2 = range(6)


def _dot(a, b):
    return jnp.dot(a, b, preferred_element_type=F32)


def _rms(x):
    return x * lax.rsqrt(jnp.mean(x * x, axis=-1, keepdims=True) + EPS)


def _gelu_tanh(x):
    inner = 0.7978845608028654 * (x + 0.044715 * (x * x * x))
    return x * (0.5 * (1.0 + jnp.tanh(inner)))


def _sigmoid(x):
    return 1.0 / (1.0 + jnp.exp(-x))


def _modulated_norm(x, gain, shift, scale):
    return _rms(x) * (gain * (1.0 + scale)) + shift


def _causal_conv3(halo_ref, cols, cur, w):
    rows = cur.shape[0]
    prev = halo_ref[:, cols]
    first = lax.broadcasted_iota(jnp.int32, (CONV_HALO, 1), 0)
    shifted = []
    for lag in (1, 2):
        rolled = pltpu.roll(cur, lag, 0)
        head = jnp.where(first < lag, pltpu.roll(prev, lag, 0), rolled[0:CONV_HALO])
        shifted.append(jnp.concatenate([head, rolled[CONV_HALO:]], axis=0))
    halo_ref[:, cols] = cur[rows - CONV_HALO:rows]
    return w[0:1] * shifted[1] + w[1:2] * shifted[0] + w[2:3] * cur


def _mod_kernel(c_ref, w_ref, b_ref, o_ref):
    c = c_ref[...]
    act = c * _sigmoid(c)
    o_ref[...] = jnp.dot(act, w_ref[...], preferred_element_type=F32,
                         precision=lax.Precision.HIGHEST) + b_ref[...]


def _ada_mod(c, ada_w, ada_b):
    depth, d, n = ada_w.shape
    bsz = c.shape[0]
    rows = 8
    c_pad = jnp.pad(c, ((0, rows - bsz), (0, 0)))
    tn = 1536
    out = pl.pallas_call(
        _mod_kernel,
        grid=(depth, n // tn),
        in_specs=[
            pl.BlockSpec((rows, d), lambda l, j: (0, 0)),
            pl.BlockSpec((None, d, tn), lambda l, j: (l, 0, j)),
            pl.BlockSpec((None, 1, tn), lambda l, j: (l, 0, j)),
        ],
        out_specs=pl.BlockSpec((None, rows, tn), lambda l, j: (l, 0, j)),
        out_shape=jax.ShapeDtypeStruct((depth, rows, n), F32),
        compiler_params=pltpu.CompilerParams(
            dimension_semantics=("arbitrary", "arbitrary"),
            vmem_limit_bytes=VMEM_LIMIT),
        name="ada_mod",
    )(c_pad, ada_w, ada_b.reshape(depth, 1, n))
    return out[:, :bsz].reshape(depth, bsz, 6, d)


def _mixer_ab_kernel(x_ref, mod_ref, g_ref, win_ref, vg_ref, vb_ref, ws_ref, bsf_ref,
                     pw_ref, ps_ref, wout_ref, o_ref, xb_ref, ycat_ref):
    t = pl.program_id(1)
    tm = x_ref.shape[0]

    @pl.when(t == 0)
    def _():
        xb_ref[0:POOL_HALO, :] = jnp.zeros((POOL_HALO, B_WIDTH), F32)

    x = x_ref[...]
    mod = mod_ref[...]
    g = g_ref[...]
    h = _modulated_norm(x, g[0:1], mod[SH1:SH1 + 1], mod[SC1:SC1 + 1]).astype(BF16)

    def gelu_proj(start):
        return jnp.concatenate(
            [_gelu_tanh(_dot(h, win_ref[:, c:c + PROJ_CHUNK]))
             for c in range(start, start + A_WIDTH, PROJ_CHUNK)], axis=1)

    gv = gelu_proj(A_WIDTH)
    xb = _dot(h, win_ref[:, 2 * A_WIDTH:2 * A_WIDTH + B_WIDTH])
    mu = jnp.mean(gv, axis=-1, keepdims=True)
    dv = gv - mu
    var = jnp.mean(dv * dv, axis=-1, keepdims=True)
    vn = (dv * lax.rsqrt(var + EPS) * vg_ref[...] + vb_ref[...]).astype(BF16)
    gu = gelu_proj(0)

    row = lax.broadcasted_iota(jnp.int32, (CHUNK, CHUNK), 0)
    col = lax.broadcasted_iota(jnp.int32, (CHUNK, CHUNK), 1)
    tril = row >= col
    for gi in range(A_GROUPS):
        cols = slice(gi * LANES, (gi + 1) * LANES)
        wg = jnp.where(tril, ws_ref[gi], 0.0).astype(BF16)
        bias = bsf_ref[:, cols]
        for ci in range(tm // CHUNK):
            rows = slice(ci * CHUNK, (ci + 1) * CHUNK)
            mixed = _dot(wg, vn[rows, cols]) + bias
            ycat_ref[rows, cols] = (gu[rows, cols] * mixed).astype(BF16)

    xb_ref[POOL_HALO:POOL_HALO + tm, :] = xb
    pos = t * tm + lax.broadcasted_iota(jnp.int32, (tm, 1), 0)
    for gi, w in enumerate(POOL_WINDOWS):
        cols = slice(gi * LANES, (gi + 1) * LANES)
        s = xb[:, cols]
        for k in range(1, w):
            s = s + xb_ref[POOL_HALO - k:POOL_HALO - k + tm, cols]
        cnt = jnp.minimum(pos + 1, w).astype(F32)
        pooled = s * (1.0 / cnt) - xb[:, cols]
        mixed = _dot(pooled.astype(BF16), pw_ref[gi]) * ps_ref[:, cols]
        ycat_ref[:, A_WIDTH + gi * LANES:A_WIDTH + (gi + 1) * LANES] = mixed.astype(BF16)
    xb_ref[0:POOL_HALO, :] = xb_ref[tm:tm + POOL_HALO, :]

    y = _dot(ycat_ref[...], wout_ref[...])
    o_ref[...] = x + mod[GT1:GT1 + 1] * (_rms(y) * g[1:2])


def _const_spec(shape):
    zeros = (0,) * len(shape)
    return pl.BlockSpec(shape, lambda b, t: zeros, pipeline_mode=pl.Buffered(1))


def _mixer_ab(x, mod, g, w_in, vnorm_g, vnorm_b, w_s, b_s, pool_w, pool_scale, w_out):
    bsz, s, d = x.shape
    tm = TOKEN_TILE
    bias_full = jnp.repeat(b_s.T, LANES, axis=1)
    return pl.pallas_call(
        _mixer_ab_kernel,
        grid=(bsz, s // tm),
        in_specs=[
            pl.BlockSpec((None, tm, d), lambda b, t: (b, t, 0)),
            pl.BlockSpec((None, 6, d), lambda b, t: (b, 0, 0)),
            _const_spec((4, d)),
            _const_spec(w_in.shape),
            _const_spec((1, A_WIDTH)),
            _const_spec((1, A_WIDTH)),
            _const_spec(w_s.shape),
            _const_spec(bias_full.shape),
            _const_spec(pool_w.shape),
            _const_spec((1, B_WIDTH)),
            _const_spec(w_out.shape),
        ],
        out_specs=pl.BlockSpec((None, tm, d), lambda b, t: (b, t, 0)),
        out_shape=jax.ShapeDtypeStruct(x.shape, F32),
        scratch_shapes=[
            pltpu.VMEM((POOL_HALO + tm, B_WIDTH), F32),
            pltpu.VMEM((tm, A_WIDTH + B_WIDTH), BF16),
        ],
        compiler_params=pltpu.CompilerParams(
            dimension_semantics=("arbitrary", "arbitrary"),
            vmem_limit_bytes=VMEM_LIMIT),
        name="mixer_ab",
    )(x, mod, g, w_in.astype(BF16), vnorm_g.reshape(1, -1), vnorm_b.reshape(1, -1),
      w_s, bias_full, pool_w.astype(BF16), pool_scale.reshape(1, -1), w_out.astype(BF16))


def _ffn_kernel(fuse_mixer_out, *refs):
    if fuse_mixer_out:
        (x_ref, yc_ref, yd_ref, wo_ref, mod_ref, g_ref, wup_ref, cw_ref, wdn_ref,
         o_ref, halo_ref, act_ref) = refs
    else:
        (x_ref, mod_ref, g_ref, wup_ref, cw_ref, wdn_ref,
         o_ref, halo_ref, act_ref) = refs
    t = pl.program_id(1)

    @pl.when(t == 0)
    def _():
        halo_ref[...] = jnp.zeros(halo_ref.shape, F32)

    x = x_ref[...]
    mod = mod_ref[...]
    g = g_ref[...]
    if fuse_mixer_out:
        y = _dot(yc_ref[...], wo_ref[0:C_WIDTH, :]) + _dot(yd_ref[...], wo_ref[C_WIDTH:, :])
        x = x + mod[GT1:GT1 + 1] * (_rms(y) * g[1:2])
    h = _modulated_norm(x, g[2:3], mod[SH2:SH2 + 1], mod[SC2:SC2 + 1]).astype(BF16)

    for j in range(D_FF // FF_CHUNK):
        halves = []
        for half in range(2):
            start = half * D_FF + j * FF_CHUNK
            cols = slice(start, start + FF_CHUNK)
            up = _dot(h, wup_ref[:, cols])
            halves.append(_causal_conv3(halo_ref, cols, up, cw_ref[:, cols]))
        gate, lin = halves
        act = gate * _sigmoid(gate) * lin
        act_ref[:, j * FF_CHUNK:(j + 1) * FF_CHUNK] = act.astype(BF16)

    y = _dot(act_ref[...], wdn_ref[...])
    o_ref[...] = x + mod[GT2:GT2 + 1] * (_rms(y) * g[3:4])


def _layer_spec(stacked_shape, layer):
    zeros = (0,) * (len(stacked_shape) - 1)
    return pl.BlockSpec((None,) + tuple(stacked_shape[1:]), lambda b, t: (layer,) + zeros,
                        pipeline_mode=pl.Buffered(1))


def _ffn(x, mod, g, layer, w_up, conv_w, w_down, mixer_out=None):
    bsz, s, d = x.shape
    tm = TOKEN_TILE
    tile = lambda width: pl.BlockSpec((None, tm, width), lambda b, t: (b, t, 0))
    in_specs = [tile(d)]
    args = [x]
    if mixer_out is not None:
        yc, yd, w_o = mixer_out
        in_specs += [tile(C_WIDTH), tile(D_WIDTH), _const_spec(w_o.shape)]
        args += [yc, yd, w_o.astype(BF16)]
    in_specs += [
        pl.BlockSpec((None, 6, d), lambda b, t: (b, 0, 0)),
        _const_spec((4, d)),
        _layer_spec(w_up.shape, layer),
        _layer_spec(conv_w.shape, layer),
        _layer_spec(w_down.shape, layer),
    ]
    args += [mod, g, w_up, conv_w, w_down]
    return pl.pallas_call(
        functools.partial(_ffn_kernel, mixer_out is not None),
        grid=(bsz, s // tm),
        in_specs=in_specs,
        out_specs=tile(d),
        out_shape=jax.ShapeDtypeStruct(x.shape, F32),
        scratch_shapes=[
            pltpu.VMEM((CONV_HALO, 2 * D_FF), F32),
            pltpu.VMEM((tm, D_FF), BF16),
        ],
        compiler_params=pltpu.CompilerParams(
            dimension_semantics=("arbitrary", "arbitrary"),
            vmem_limit_bytes=VMEM_LIMIT),
        name="conv_ffn_fused" if mixer_out is not None else "conv_ffn",
    )(*args)


def _mixer_cd_in_kernel(x_ref, mod_ref, g_ref, pos_ref, invf_ref, win_ref, cw_ref,
                        yc_ref, qt_ref, k_ref, vt_ref, halo_ref):
    t = pl.program_id(1)
    tm = x_ref.shape[0]

    @pl.when(t == 0)
    def _():
        halo_ref[...] = jnp.zeros(halo_ref.shape, F32)

    mod = mod_ref[...]
    g = g_ref[...]
    h = _modulated_norm(x_ref[...], g[0:1], mod[SH1:SH1 + 1], mod[SC1:SC1 + 1]).astype(BF16)

    def proj(i):
        return _dot(h, win_ref[:, i * C_WIDTH:(i + 1) * C_WIDTH])

    prod = proj(1) * proj(2)
    conv = _causal_conv3(halo_ref, slice(0, C_WIDTH), prod, cw_ref[...])
    yc_ref[...] = (proj(0) * conv).astype(BF16)

    ang = pos_ref[...].astype(F32) * invf_ref[...]
    lane = lax.broadcasted_iota(jnp.int32, (1, LANES), 1) % HEAD_DIM
    half = ROT_DIM // 2
    cos = jnp.where(lane < ROT_DIM, jnp.cos(ang), 1.0)
    sin = jnp.sin(ang)
    sin = jnp.where(lane < half, -sin, jnp.where(lane < ROT_DIM, sin, 0.0))

    def rope(v):
        outs = []
        for ci in range(v.shape[1] // LANES):
            vc = v[:, ci * LANES:(ci + 1) * LANES]
            partner = jnp.where(lane < half, pltpu.roll(vc, LANES - half, 1),
                                pltpu.roll(vc, half, 1))
            outs.append(vc * cos + partner * sin)
        return jnp.concatenate(outs, axis=1)

    q = rope(proj(3)) * QK_SCALE
    k = rope(proj(4))
    v = proj(5)
    ones_rows = jnp.where(
        lax.broadcasted_iota(jnp.int32, (VT_ROWS - HEAD_DIM, MOBA_BLOCK), 0) == 0, 1.0, 0.0)
    for bi in range(tm // MOBA_BLOCK):
        rows = slice(bi * MOBA_BLOCK, (bi + 1) * MOBA_BLOCK)
        qt_ref[bi] = q[rows].T.astype(BF16)
        v_t = v[rows].T
        for hd in range(D_HEADS):
            vt_ref[bi, hd] = jnp.concatenate(
                [v_t[hd * HEAD_DIM:(hd + 1) * HEAD_DIM], ones_rows], axis=0).astype(BF16)
    k_ref[...] = k.astype(BF16)


def _mixer_cd_in(x, mod, g, positions, w_in, conv_w):
    bsz, s, d = x.shape
    tm = TOKEN_TILE
    nb = s // MOBA_BLOCK
    bpt = tm // MOBA_BLOCK
    lane = jnp.arange(LANES) % HEAD_DIM
    inv_freq = ROPE_THETA ** (-jnp.arange(0, ROT_DIM, 2, dtype=F32) / ROT_DIM)
    invf = jnp.where(lane < ROT_DIM, inv_freq[lane % (ROT_DIM // 2)], 0.0).reshape(1, LANES)
    yc, qt, k, vt = pl.pallas_call(
        _mixer_cd_in_kernel,
        grid=(bsz, s // tm),
        in_specs=[
            pl.BlockSpec((None, tm, d), lambda b, t: (b, t, 0)),
            pl.BlockSpec((None, 6, d), lambda b, t: (b, 0, 0)),
            _const_spec((4, d)),
            pl.BlockSpec((None, tm, 1), lambda b, t: (b, t, 0)),
            _const_spec((1, LANES)),
            _const_spec(w_in.shape),
            _const_spec(conv_w.shape),
        ],
        out_specs=[
            pl.BlockSpec((None, tm, C_WIDTH), lambda b, t: (b, t, 0)),
            pl.BlockSpec((None, bpt, D_WIDTH, MOBA_BLOCK), lambda b, t: (b, t, 0, 0)),
            pl.BlockSpec((None, tm, D_WIDTH), lambda b, t: (b, t, 0)),
            pl.BlockSpec((None, bpt, D_HEADS, VT_ROWS, MOBA_BLOCK), lambda b, t: (b, t, 0, 0, 0)),
        ],
        out_shape=[
            jax.ShapeDtypeStruct((bsz, s, C_WIDTH), BF16),
            jax.ShapeDtypeStruct((bsz, nb, D_WIDTH, MOBA_BLOCK), BF16),
            jax.ShapeDtypeStruct((bsz, s, D_WIDTH), BF16),
            jax.ShapeDtypeStruct((bsz, nb, D_HEADS, VT_ROWS, MOBA_BLOCK), BF16),
        ],
        scratch_shapes=[
            pltpu.VMEM((CONV_HALO, C_WIDTH), F32),
        ],
        compiler_params=pltpu.CompilerParams(
            dimension_semantics=("arbitrary", "arbitrary"),
            vmem_limit_bytes=VMEM_LIMIT),
        name="mixer_cd_in",
    )(x, mod, g, positions.reshape(bsz, s, 1), invf, w_in.astype(BF16), conv_w)
    return yc, qt, k.reshape(bsz, nb, MOBA_BLOCK, D_WIDTH), vt


def _moba_kernel(qt_ref, k_ref, vt_ref, o_ref, kmean_ref):
    qi = pl.program_id(2)
    nb = k_ref.shape[0]
    width, tq = qt_ref.shape
    heads = width // HEAD_DIM
    neg_inf = -jnp.inf

    @pl.when(qi == 0)
    def _():
        for n in range(nb):
            kmean_ref[n:n + 1, :] = (jnp.sum(k_ref[n].astype(F32), axis=0, keepdims=True)
                                     * (1.0 / MOBA_BLOCK))

    def lanes_of(hd):
        return slice(hd // 2 * LANES, (hd // 2 + 1) * LANES)

    def weights(st, shift):
        return jnp.exp2((st - shift).astype(BF16))

    def tile(n_past):
        groups = [(start, min(KV_GROUP, n_past - start)) for start in range(0, n_past, KV_GROUP)]
        qt = qt_ref[...]
        slab_head = lax.broadcasted_iota(jnp.int32, (LANES, 1), 0) // HEAD_DIM
        lane_head = lax.broadcasted_iota(jnp.int32, (1, width), 1) // HEAD_DIM
        blk = lax.broadcasted_iota(jnp.int32, (nb, tq), 0)
        kpos = lax.broadcasted_iota(jnp.int32, (MOBA_BLOCK, tq), 0)
        qpos = lax.broadcasted_iota(jnp.int32, (MOBA_BLOCK, tq), 1)
        causal = kpos <= qpos
        valid = blk < qi

        qs = [jnp.where(slab_head == hd % 2, qt[lanes_of(hd)], jnp.zeros((LANES, tq), BF16))
              for hd in range(heads)]

        def head_scores(group, hd):
            start, size = group
            return [_dot(k_ref[start + u, :, lanes_of(hd)], qs[hd]) for u in range(size)]

        kmean = kmean_ref[...]
        km_heads = jnp.concatenate(
            [jnp.where(lane_head == hd, kmean, 0.0) for hd in range(heads)], axis=0)
        gates = jnp.dot(km_heads, qt.astype(F32), preferred_element_type=F32,
                        precision=lax.Precision.HIGHEST)
        own = [_dot(k_ref[qi, :, lanes_of(hd)], qs[hd]) for hd in range(heads)]
        nxt = [head_scores(groups[0], hd) for hd in range(heads)] if groups else None

        picks = []
        for hd in range(heads):
            gate = jnp.where(valid, gates[hd * nb:(hd + 1) * nb], neg_inf)
            beats = []
            for m in range(nb):
                gm = gate[m:m + 1, :]
                tie = jnp.where(blk > m, 1, 0)
                beats.append(jnp.where(gm > gate, 1, jnp.where(gm == gate, tie, 0)))
            while len(beats) > 1:
                beats = [a + b for a, b in zip(beats[0::2], beats[1::2])]
            picks.append(jnp.where(valid, jnp.where(beats[0] < MOBA_TOPK, 1.0, 0.0), 0.0))

        state = []
        for hd in range(heads):
            st = jnp.where(causal, own[hd], neg_inf)
            m0 = jnp.max(st, axis=0, keepdims=True)
            state.append((m0, _dot(vt_ref[qi, hd], weights(st, m0))))

        for gi, (start, size) in enumerate(groups):
            cur = nxt
            nxt = []
            for hd in range(heads):
                if gi + 1 < len(groups):
                    nxt.append(head_scores(groups[gi + 1], hd))
                m_run, acc = state[hd]
                picked = []
                m_new = m_run
                for u in range(size):
                    pick = picks[hd][start + u:start + u + 1, :] > 0.5
                    col_max = jnp.max(cur[hd][u], axis=0, keepdims=True)
                    m_new = jnp.maximum(m_new, jnp.where(pick, col_max, neg_inf))
                    picked.append(pick)
                acc = jnp.exp2(m_run - m_new) * acc
                for u in range(size):
                    p = weights(cur[hd][u], jnp.where(picked[u], m_new, jnp.inf))
                    acc = acc + _dot(vt_ref[start + u, hd], p)
                state[hd] = (m_new, acc)

        out_t = jnp.concatenate(
            [acc[0:HEAD_DIM, :] * (1.0 / acc[HEAD_DIM:HEAD_DIM + 1, :]) for _, acc in state],
            axis=0)
        o_ref[...] = out_t.T.astype(BF16)

    steps = (qi + KV_STEP - 1) // KV_STEP
    for count in range(nb // KV_STEP + 1):
        pl.when(steps == count)(functools.partial(tile, count * KV_STEP))


def _moba_attention(qt, k, vt):
    bsz, nb, dw, blk = qt.shape
    heads = HEADS_PER_STEP
    width = heads * HEAD_DIM
    s = nb * blk
    return pl.pallas_call(
        _moba_kernel,
        grid=(bsz, dw // width, nb),
        in_specs=[
            pl.BlockSpec((None, None, width, blk), lambda b, p, i: (b, i, p, 0)),
            pl.BlockSpec((None, nb, blk, width), lambda b, p, i: (b, 0, 0, p)),
            pl.BlockSpec((None, nb, heads, VT_ROWS, blk), lambda b, p, i: (b, 0, p, 0, 0)),
        ],
        out_specs=pl.BlockSpec((None, blk, width), lambda b, p, i: (b, i, p)),
        out_shape=jax.ShapeDtypeStruct((bsz, s, dw), BF16),
        scratch_shapes=[pltpu.VMEM((nb, width), F32)],
        compiler_params=pltpu.CompilerParams(
            dimension_semantics=("arbitrary", "arbitrary", "arbitrary"),
            vmem_limit_bytes=VMEM_LIMIT),
        name="moba_attention",
    )(qt, k, vt)


def kernel(x, c, positions, ab_w_in, ab_vnorm_g, ab_vnorm_b, ab_spatial_w, ab_spatial_b,
           ab_pool_w, ab_pool_scale, ab_w_out, cd_w_in, cd_conv_w, cd_w_out,
           ffn_w_up, ffn_conv_w, ffn_w_down, ada_w, ada_b, norm_g):
    depth = ada_w.shape[0]
    mod = _ada_mod(c, ada_w, ada_b)
    ffn = (ffn_w_up.astype(BF16), ffn_conv_w, ffn_w_down.astype(BF16))
    for i in range(depth):
        j = i // 2
        if i % 2 == 0:
            x = _mixer_ab(x, mod[i], norm_g[i], ab_w_in[j], ab_vnorm_g[j], ab_vnorm_b[j],
                          ab_spatial_w[j], ab_spatial_b[j], ab_pool_w[j], ab_pool_scale[j],
                          ab_w_out[j])
            x = _ffn(x, mod[i], norm_g[i], i, *ffn)
        else:
            yc, qt, k, vt = _mixer_cd_in(x, mod[i], norm_g[i], positions, cd_w_in[j], cd_conv_w[j])
            yd = _moba_attention(qt, k, vt)
            x = _ffn(x, mod[i], norm_g[i], i, *ffn, mixer_out=(yc, yd, cd_w_out[j]))
    return x
```

```python
import functools

import jax
import jax.numpy as jnp
from jax import lax
from jax.experimental import pallas as pl
from jax.experimental.pallas import tpu as pltpu

F32 = jnp.float32
BF16 = jnp.bfloat16

D_MODEL = 1024
CHUNK = 128
A_GROUPS = 8
A_WIDTH = 1024
POOL_WINDOWS = (2, 4, 8, 16)
B_GROUP_DIM = 128
B_WIDTH = len(POOL_WINDOWS) * B_GROUP_DIM
POOL_HALO = 16
C_WIDTH = 512
CONV_WIDTH = 3
CONV_HALO = 8
D_HEADS = 8
HEAD_DIM = 64
D_WIDTH = D_HEADS * HEAD_DIM
ROT_DIM = HEAD_DIM // 4
ROPE_THETA = 500000.0
MOBA_BLOCK = 256
MOBA_TOPK = 3
D_FF = 2816
EPS = 1e-6

LANES = 128
TOKEN_TILE = 1024
FF_CHUNK = 256
PROJ_CHUNK = 256
KV_GROUP = 1
KV_STEP = 2
HEADS_PER_STEP = 4
VT_ROWS = HEAD_DIM + 16
QK_SCALE = HEAD_DIM ** -0.5 * 1.4426950408889634
VMEM_LIMIT = 56 * 1024 * 1024

SH1, SC1, GT1, SH2, SC2, GT2 = range(6)


def _dot(a, b):
    return jnp.dot(a, b, preferred_element_type=F32)


def _rms(x):
    return x * lax.rsqrt(jnp.mean(x * x, axis=-1, keepdims=True) + EPS)


def _gelu_tanh(x):
    inner = 0.7978845608028654 * (x + 0.044715 * (x * x * x))
    return x * (0.5 * (1.0 + jnp.tanh(inner)))


def _sigmoid(x):
    return 1.0 / (1.0 + jnp.exp(-x))


def _modulated_norm(x, gain, shift, scale):
    return _rms(x) * (gain * (1.0 + scale)) + shift


def _causal_conv3(halo_ref, cols, cur, w):
    rows = cur.shape[0]
    prev = halo_ref[:, cols]
    first = lax.broadcasted_iota(jnp.int32, (CONV_HALO, 1), 0)
    shifted = []
    for lag in (1, 2):
        rolled = pltpu.roll(cur, lag, 0)
        head = jnp.where(first < lag, pltpu.roll(prev, lag, 0), rolled[0:CONV_HALO])
        shifted.append(jnp.concatenate([head, rolled[CONV_HALO:]], axis=0))
    halo_ref[:, cols] = cur[rows - CONV_HALO:rows]
    return w[0:1] * shifted[1] + w[1:2] * shifted[0] + w[2:3] * cur


def _mod_kernel(c_ref, w_ref, b_ref, o_ref):
    c = c_ref[...]
    act = c * _sigmoid(c)
    o_ref[...] = jnp.dot(act, w_ref[...], preferred_element_type=F32,
                         precision=lax.Precision.HIGHEST) + b_ref[...]


def _ada_mod(c, ada_w, ada_b):
    depth, d, n = ada_w.shape
    bsz = c.shape[0]
    rows = 8
    c_pad = jnp.pad(c, ((0, rows - bsz), (0, 0)))
    tn = 1536
    out = pl.pallas_call(
        _mod_kernel,
        grid=(depth, n // tn),
        in_specs=[
            pl.BlockSpec((rows, d), lambda l, j: (0, 0)),
            pl.BlockSpec((None, d, tn), lambda l, j: (l, 0, j)),
            pl.BlockSpec((None, 1, tn), lambda l, j: (l, 0, j)),
        ],
        out_specs=pl.BlockSpec((None, rows, tn), lambda l, j: (l, 0, j)),
        out_shape=jax.ShapeDtypeStruct((depth, rows, n), F32),
        compiler_params=pltpu.CompilerParams(
            dimension_semantics=("arbitrary", "arbitrary"),
            vmem_limit_bytes=VMEM_LIMIT),
        name="ada_mod",
    )(c_pad, ada_w, ada_b.reshape(depth, 1, n))
    return out[:, :bsz].reshape(depth, bsz, 6, d)


def _mixer_ab_kernel(x_ref, mod_ref, g_ref, win_ref, vg_ref, vb_ref, ws_ref, bsf_ref,
                     pw_ref, ps_ref, wout_ref, o_ref, xb_ref, ycat_ref):
    t = pl.program_id(1)
    tm = x_ref.shape[0]

    @pl.when(t == 0)
    def _():
        xb_ref[0:POOL_HALO, :] = jnp.zeros((POOL_HALO, B_WIDTH), F32)

    x = x_ref[...]
    mod = mod_ref[...]
    g = g_ref[...]
    h = _modulated_norm(x, g[0:1], mod[SH1:SH1 + 1], mod[SC1:SC1 + 1]).astype(BF16)

    def gelu_proj(start):
        return jnp.concatenate(
            [_gelu_tanh(_dot(h, win_ref[:, c:c + PROJ_CHUNK]))
             for c in range(start, start + A_WIDTH, PROJ_CHUNK)], axis=1)

    gv = gelu_proj(A_WIDTH)
    xb = _dot(h, win_ref[:, 2 * A_WIDTH:2 * A_WIDTH + B_WIDTH])
    mu = jnp.mean(gv, axis=-1, keepdims=True)
    dv = gv - mu
    var = jnp.mean(dv * dv, axis=-1, keepdims=True)
    vn = (dv * lax.rsqrt(var + EPS) * vg_ref[...] + vb_ref[...]).astype(BF16)
    gu = gelu_proj(0)

    row = lax.broadcasted_iota(jnp.int32, (CHUNK, CHUNK), 0)
    col = lax.broadcasted_iota(jnp.int32, (CHUNK, CHUNK), 1)
    tril = row >= col
    for gi in range(A_GROUPS):
        cols = slice(gi * LANES, (gi + 1) * LANES)
        wg = jnp.where(tril, ws_ref[gi], 0.0).astype(BF16)
        bias = bsf_ref[:, cols]
        for ci in range(tm // CHUNK):
            rows = slice(ci * CHUNK, (ci + 1) * CHUNK)
            mixed = _dot(wg, vn[rows, cols]) + bias
            ycat_ref[rows, cols] = (gu[rows, cols] * mixed).astype(BF16)

    xb_ref[POOL_HALO:POOL_HALO + tm, :] = xb
    pos = t * tm + lax.broadcasted_iota(jnp.int32, (tm, 1), 0)
    for gi, w in enumerate(POOL_WINDOWS):
        cols = slice(gi * LANES, (gi + 1) * LANES)
        s = xb[:, cols]
        for k in range(1, w):
            s = s + xb_ref[POOL_HALO - k:POOL_HALO - k + tm, cols]
        cnt = jnp.minimum(pos + 1, w).astype(F32)
        pooled = s * (1.0 / cnt) - xb[:, cols]
        mixed = _dot(pooled.astype(BF16), pw_ref[gi]) * ps_ref[:, cols]
        ycat_ref[:, A_WIDTH + gi * LANES:A_WIDTH + (gi + 1) * LANES] = mixed.astype(BF16)
    xb_ref[0:POOL_HALO, :] = xb_ref[tm:tm + POOL_HALO, :]

    y = _dot(ycat_ref[...], wout_ref[...])
    o_ref[...] = x + mod[GT1:GT1 + 1] * (_rms(y) * g[1:2])


def _const_spec(shape):
    zeros = (0,) * len(shape)
    return pl.BlockSpec(shape, lambda b, t: zeros, pipeline_mode=pl.Buffered(1))


def _mixer_ab(x, mod, g, w_in, vnorm_g, vnorm_b, w_s, b_s, pool_w, pool_scale, w_out):
    bsz, s, d = x.shape
    tm = TOKEN_TILE
    bias_full = jnp.repeat(b_s.T, LANES, axis=1)
    return pl.pallas_call(
        _mixer_ab_kernel,
        grid=(bsz, s // tm),
        in_specs=[
            pl.BlockSpec((None, tm, d), lambda b, t: (b, t, 0)),
            pl.BlockSpec((None, 6, d), lambda b, t: (b, 0, 0)),
            _const_spec((4, d)),
            _const_spec(w_in.shape),
            _const_spec((1, A_WIDTH)),
            _const_spec((1, A_WIDTH)),
            _const_spec(w_s.shape),
            _const_spec(bias_full.shape),
            _const_spec(pool_w.shape),
            _const_spec((1, B_WIDTH)),
            _const_spec(w_out.shape),
        ],
        out_specs=pl.BlockSpec((None, tm, d), lambda b, t: (b, t, 0)),
        out_shape=jax.ShapeDtypeStruct(x.shape, F32),
        scratch_shapes=[
            pltpu.VMEM((POOL_HALO + tm, B_WIDTH), F32),
            pltpu.VMEM((tm, A_WIDTH + B_WIDTH), BF16),
        ],
        compiler_params=pltpu.CompilerParams(
            dimension_semantics=("arbitrary", "arbitrary"),
            vmem_limit_bytes=VMEM_LIMIT),
        name="mixer_ab",
    )(x, mod, g, w_in.astype(BF16), vnorm_g.reshape(1, -1), vnorm_b.reshape(1, -1),
      w_s, bias_full, pool_w.astype(BF16), pool_scale.reshape(1, -1), w_out.astype(BF16))


def _ffn_kernel(fuse_mixer_out, *refs):
    if fuse_mixer_out:
        (x_ref, yc_ref, yd_ref, wo_ref, mod_ref, g_ref, wup_ref, cw_ref, wdn_ref,
         o_ref, halo_ref, act_ref) = refs
    else:
        (x_ref, mod_ref, g_ref, wup_ref, cw_ref, wdn_ref,
         o_ref, halo_ref, act_ref) = refs
    t = pl.program_id(1)

    @pl.when(t == 0)
    def _():
        halo_ref[...] = jnp.zeros(halo_ref.shape, F32)

    x = x_ref[...]
    mod = mod_ref[...]
    g = g_ref[...]
    if fuse_mixer_out:
        y = _dot(yc_ref[...], wo_ref[0:C_WIDTH, :]) + _dot(yd_ref[...], wo_ref[C_WIDTH:, :])
        x = x + mod[GT1:GT1 + 1] * (_rms(y) * g[1:2])
    h = _modulated_norm(x, g[2:3], mod[SH2:SH2 + 1], mod[SC2:SC2 + 1]).astype(BF16)

    for j in range(D_FF // FF_CHUNK):
        halves = []
        for half in range(2):
            start = half * D_FF + j * FF_CHUNK
            cols = slice(start, start + FF_CHUNK)
            up = _dot(h, wup_ref[:, cols])
            halves.append(_causal_conv3(halo_ref, cols, up, cw_ref[:, cols]))
        gate, lin = halves
        act = gate * _sigmoid(gate) * lin
        act_ref[:, j * FF_CHUNK:(j + 1) * FF_CHUNK] = act.astype(BF16)

    y = _dot(act_ref[...], wdn_ref[...])
    o_ref[...] = x + mod[GT2:GT2 + 1] * (_rms(y) * g[3:4])


def _layer_spec(stacked_shape, layer):
    zeros = (0,) * (len(stacked_shape) - 1)
    return pl.BlockSpec((None,) + tuple(stacked_shape[1:]), lambda b, t: (layer,) + zeros,
                        pipeline_mode=pl.Buffered(1))


def _ffn(x, mod, g, layer, w_up, conv_w, w_down, mixer_out=None):
    bsz, s, d = x.shape
    tm = TOKEN_TILE
    tile = lambda width: pl.BlockSpec((None, tm, width), lambda b, t: (b, t, 0))
    in_specs = [tile(d)]
    args = [x]
    if mixer_out is not None:
        yc, yd, w_o = mixer_out
        in_specs += [tile(C_WIDTH), tile(D_WIDTH), _const_spec(w_o.shape)]
        args += [yc, yd, w_o.astype(BF16)]
    in_specs += [
        pl.BlockSpec((None, 6, d), lambda b, t: (b, 0, 0)),
        _const_spec((4, d)),
        _layer_spec(w_up.shape, layer),
        _layer_spec(conv_w.shape, layer),
        _layer_spec(w_down.shape, layer),
    ]
    args += [mod, g, w_up, conv_w, w_down]
    return pl.pallas_call(
        functools.partial(_ffn_kernel, mixer_out is not None),
        grid=(bsz, s // tm),
        in_specs=in_specs,
        out_specs=tile(d),
        out_shape=jax.ShapeDtypeStruct(x.shape, F32),
        scratch_shapes=[
            pltpu.VMEM((CONV_HALO, 2 * D_FF), F32),
            pltpu.VMEM((tm, D_FF), BF16),
        ],
        compiler_params=pltpu.CompilerParams(
            dimension_semantics=("arbitrary", "arbitrary"),
            vmem_limit_bytes=VMEM_LIMIT),
        name="conv_ffn_fused" if mixer_out is not None else "conv_ffn",
    )(*args)


def _mixer_cd_in_kernel(x_ref, mod_ref, g_ref, pos_ref, invf_ref, win_ref, cw_ref,
                        yc_ref, qt_ref, k_ref, vt_ref, halo_ref):
    t = pl.program_id(1)
    tm = x_ref.shape[0]

    @pl.when(t == 0)
    def _():
        halo_ref[...] = jnp.zeros(halo_ref.shape, F32)

    mod = mod_ref[...]
    g = g_ref[...]
    h = _modulated_norm(x_ref[...], g[0:1], mod[SH1:SH1 + 1], mod[SC1:SC1 + 1]).astype(BF16)

    def proj(i):
        return _dot(h, win_ref[:, i * C_WIDTH:(i + 1) * C_WIDTH])

    prod = proj(1) * proj(2)
    conv = _causal_conv3(halo_ref, slice(0, C_WIDTH), prod, cw_ref[...])
    yc_ref[...] = (proj(0) * conv).astype(BF16)

    ang = pos_ref[...].astype(F32) * invf_ref[...]
    lane = lax.broadcasted_iota(jnp.int32, (1, LANES), 1) % HEAD_DIM
    half = ROT_DIM // 2
    cos = jnp.where(lane < ROT_DIM, jnp.cos(ang), 1.0)
    sin = jnp.sin(ang)
    sin = jnp.where(lane < half, -sin, jnp.where(lane < ROT_DIM, sin, 0.0))

    def rope(v):
        outs = []
        for ci in range(v.shape[1] // LANES):
            vc = v[:, ci * LANES:(ci + 1) * LANES]
            partner = jnp.where(lane < half, pltpu.roll(vc, LANES - half, 1),
                                pltpu.roll(vc, half, 1))
            outs.append(vc * cos + partner * sin)
        return jnp.concatenate(outs, axis=1)

    q = rope(proj(3)) * QK_SCALE
    k = rope(proj(4))
    v = proj(5)
    ones_rows = jnp.where(
        lax.broadcasted_iota(jnp.int32, (VT_ROWS - HEAD_DIM, MOBA_BLOCK), 0) == 0, 1.0, 0.0)
    for bi in range(tm // MOBA_BLOCK):
        rows = slice(bi * MOBA_BLOCK, (bi + 1) * MOBA_BLOCK)
        qt_ref[bi] = q[rows].T.astype(BF16)
        v_t = v[rows].T
        for hd in range(D_HEADS):
            vt_ref[bi, hd] = jnp.concatenate(
                [v_t[hd * HEAD_DIM:(hd + 1) * HEAD_DIM], ones_rows], axis=0).astype(BF16)
    k_ref[...] = k.astype(BF16)


def _mixer_cd_in(x, mod, g, positions, w_in, conv_w):
    bsz, s, d = x.shape
    tm = TOKEN_TILE
    nb = s // MOBA_BLOCK
    bpt = tm // MOBA_BLOCK
    lane = jnp.arange(LANES) % HEAD_DIM
    inv_freq = ROPE_THETA ** (-jnp.arange(0, ROT_DIM, 2, dtype=F32) / ROT_DIM)
    invf = jnp.where(lane < ROT_DIM, inv_freq[lane % (ROT_DIM // 2)], 0.0).reshape(1, LANES)
    yc, qt, k, vt = pl.pallas_call(
        _mixer_cd_in_kernel,
        grid=(bsz, s // tm),
        in_specs=[
            pl.BlockSpec((None, tm, d), lambda b, t: (b, t, 0)),
            pl.BlockSpec((None, 6, d), lambda b, t: (b, 0, 0)),
            _const_spec((4, d)),
            pl.BlockSpec((None, tm, 1), lambda b, t: (b, t, 0)),
            _const_spec((1, LANES)),
            _const_spec(w_in.shape),
            _const_spec(conv_w.shape),
        ],
        out_specs=[
            pl.BlockSpec((None, tm, C_WIDTH), lambda b, t: (b, t, 0)),
            pl.BlockSpec((None, bpt, D_WIDTH, MOBA_BLOCK), lambda b, t: (b, t, 0, 0)),
            pl.BlockSpec((None, tm, D_WIDTH), lambda b, t: (b, t, 0)),
            pl.BlockSpec((None, bpt, D_HEADS, VT_ROWS, MOBA_BLOCK), lambda b, t: (b, t, 0, 0, 0)),
        ],
        out_shape=[
            jax.ShapeDtypeStruct((bsz, s, C_WIDTH), BF16),
            jax.ShapeDtypeStruct((bsz, nb, D_WIDTH, MOBA_BLOCK), BF16),
            jax.ShapeDtypeStruct((bsz, s, D_WIDTH), BF16),
            jax.ShapeDtypeStruct((bsz, nb, D_HEADS, VT_ROWS, MOBA_BLOCK), BF16),
        ],
        scratch_shapes=[
            pltpu.VMEM((CONV_HALO, C_WIDTH), F32),
        ],
        compiler_params=pltpu.CompilerParams(
            dimension_semantics=("arbitrary", "arbitrary"),
            vmem_limit_bytes=VMEM_LIMIT),
        name="mixer_cd_in",
    )(x, mod, g, positions.reshape(bsz, s, 1), invf, w_in.astype(BF16), conv_w)
    return yc, qt, k.reshape(bsz, nb, MOBA_BLOCK, D_WIDTH), vt


def _moba_kernel(qt_ref, k_ref, vt_ref, o_ref, kmean_ref):
    qi = pl.program_id(2)
    nb = k_ref.shape[0]
    width, tq = qt_ref.shape
    heads = width // HEAD_DIM
    neg_inf = -jnp.inf

    @pl.when(qi == 0)
    def _():
        for n in range(nb):
            kmean_ref[n:n + 1, :] = (jnp.sum(k_ref[n].astype(F32), axis=0, keepdims=True)
                                     * (1.0 / MOBA_BLOCK))

    def lanes_of(hd):
        return slice(hd // 2 * LANES, (hd // 2 + 1) * LANES)

    def weights(st, shift):
        return jnp.exp2((st - shift).astype(BF16))

    def tile(n_past):
        groups = [(start, min(KV_GROUP, n_past - start)) for start in range(0, n_past, KV_GROUP)]
        qt = qt_ref[...]
        slab_head = lax.broadcasted_iota(jnp.int32, (LANES, 1), 0) // HEAD_DIM
        lane_head = lax.broadcasted_iota(jnp.int32, (1, width), 1) // HEAD_DIM
        blk = lax.broadcasted_iota(jnp.int32, (nb, tq), 0)
        kpos = lax.broadcasted_iota(jnp.int32, (MOBA_BLOCK, tq), 0)
        qpos = lax.broadcasted_iota(jnp.int32, (MOBA_BLOCK, tq), 1)
        causal = kpos <= qpos
        valid = blk < qi

        qs = [jnp.where(slab_head == hd % 2, qt[lanes_of(hd)], jnp.zeros((LANES, tq), BF16))
              for hd in range(heads)]

        def head_scores(group, hd):
            start, size = group
            return [_dot(k_ref[start + u, :, lanes_of(hd)], qs[hd]) for u in range(size)]

        kmean = kmean_ref[...]
        km_heads = jnp.concatenate(
            [jnp.where(lane_head == hd, kmean, 0.0) for hd in range(heads)], axis=0)
        gates = jnp.dot(km_heads, qt.astype(F32), preferred_element_type=F32,
                        precision=lax.Precision.HIGHEST)
        own = [_dot(k_ref[qi, :, lanes_of(hd)], qs[hd]) for hd in range(heads)]
        nxt = [head_scores(groups[0], hd) for hd in range(heads)] if groups else None

        picks = []
        for hd in range(heads):
            gate = jnp.where(valid, gates[hd * nb:(hd + 1) * nb], neg_inf)
            beats = []
            for m in range(nb):
                gm = gate[m:m + 1, :]
                tie = jnp.where(blk > m, 1, 0)
                beats.append(jnp.where(gm > gate, 1, jnp.where(gm == gate, tie, 0)))
            while len(beats) > 1:
                beats = [a + b for a, b in zip(beats[0::2], beats[1::2])]
            picks.append(jnp.where(valid, jnp.where(beats[0] < MOBA_TOPK, 1.0, 0.0), 0.0))

        state = []
        for hd in range(heads):
            st = jnp.where(causal, own[hd], neg_inf)
            m0 = jnp.max(st, axis=0, keepdims=True)
            state.append((m0, _dot(vt_ref[qi, hd], weights(st, m0))))

        for gi, (start, size) in enumerate(groups):
            cur = nxt
            nxt = []
            for hd in range(heads):
                if gi + 1 < len(groups):
                    nxt.append(head_scores(groups[gi + 1], hd))
                m_run, acc = state[hd]
                picked = []
                m_new = m_run
                for u in range(size):
                    pick = picks[hd][start + u:start + u + 1, :] > 0.5
                    col_max = jnp.max(cur[hd][u], axis=0, keepdims=True)
                    m_new = jnp.maximum(m_new, jnp.where(pick, col_max, neg_inf))
                    picked.append(pick)
                acc = jnp.exp2(m_run - m_new) * acc
                for u in range(size):
                    p = weights(cur[hd][u], jnp.where(picked[u], m_new, jnp.inf))
                    acc = acc + _dot(vt_ref[start + u, hd], p)
                state[hd] = (m_new, acc)

        out_t = jnp.concatenate(
            [acc[0:HEAD_DIM, :] * (1.0 / acc[HEAD_DIM:HEAD_DIM + 1, :]) for _, acc in state],
            axis=0)
        o_ref[...] = out_t.T.astype(BF16)

    steps = (qi + KV_STEP - 1) // KV_STEP
    for count in range(nb // KV_STEP + 1):
        pl.when(steps == count)(functools.partial(tile, count * KV_STEP))


def _moba_attention(qt, k, vt):
    bsz, nb, dw, blk = qt.shape
    heads = HEADS_PER_STEP
    width = heads * HEAD_DIM
    s = nb * blk
    return pl.pallas_call(
        _moba_kernel,
        grid=(bsz, dw // width, nb),
        in_specs=[
            pl.BlockSpec((None, None, width, blk), lambda b, p, i: (b, i, p, 0)),
            pl.BlockSpec((None, nb, blk, width), lambda b, p, i: (b, 0, 0, p)),
            pl.BlockSpec((None, nb, heads, VT_ROWS, blk), lambda b, p, i: (b, 0, p, 0, 0)),
        ],
        out_specs=pl.BlockSpec((None, blk, width), lambda b, p, i: (b, i, p)),
        out_shape=jax.ShapeDtypeStruct((bsz, s, dw), BF16),
        scratch_shapes=[pltpu.VMEM((nb, width), F32)],
        compiler_params=pltpu.CompilerParams(
            dimension_semantics=("arbitrary", "arbitrary", "arbitrary"),
            vmem_limit_bytes=VMEM_LIMIT),
        name="moba_attention",
    )(qt, k, vt)


def kernel(x, c, positions, ab_w_in, ab_vnorm_g, ab_vnorm_b, ab_spatial_w, ab_spatial_b,
           ab_pool_w, ab_pool_scale, ab_w_out, cd_w_in, cd_conv_w, cd_w_out,
           ffn_w_up, ffn_conv_w, ffn_w_down, ada_w, ada_b, norm_g):
    depth = ada_w.shape[0]
    mod = _ada_mod(c, ada_w, ada_b)
    ffn = (ffn_w_up.astype(BF16), ffn_conv_w, ffn_w_down.astype(BF16))
    for i in range(depth):
        j = i // 2
        if i % 2 == 0:
            x = _mixer_ab(x, mod[i], norm_g[i], ab_w_in[j], ab_vnorm_g[j], ab_vnorm_b[j],
                          ab_spatial_w[j], ab_spatial_b[j], ab_pool_w[j], ab_pool_scale[j],
                          ab_w_out[j])
            x = _ffn(x, mod[i], norm_g[i], i, *ffn)
        else:
            yc, qt, k, vt = _mixer_cd_in(x, mod[i], norm_g[i], positions, cd_w_in[j], cd_conv_w[j])
            yd = _moba_attention(qt, k, vt)
            x = _ffn(x, mod[i], norm_g[i], i, *ffn, mixer_out=(yc, yd, cd_w_out[j]))
    return x
```

```python
import functools

import jax
import jax.numpy as jnp
from jax import lax
from jax.experimental import pallas as pl
from jax.experimental.pallas import tpu as pltpu

F32 = jnp.float32
BF16 = jnp.bfloat16

D_MODEL = 1024
CHUNK = 128
A_GROUPS = 8
A_WIDTH = 1024
POOL_WINDOWS = (2, 4, 8, 16)
B_GROUP_DIM = 128
B_WIDTH = len(POOL_WINDOWS) * B_GROUP_DIM
POOL_HALO = 16
C_WIDTH = 512
CONV_WIDTH = 3
CONV_HALO = 8
D_HEADS = 8
HEAD_DIM = 64
D_WIDTH = D_HEADS * HEAD_DIM
ROT_DIM = HEAD_DIM // 4
ROPE_THETA = 500000.0
POS_PER_ROW = 128 // ROT_DIM
MOBA_BLOCK = 256
MOBA_TOPK = 3
D_FF = 2816
EPS = 1e-6

LANES = 128
TOKEN_TILE = 1024
FF_CHUNK = 256
PROJ_CHUNK = 256
KV_GROUP = 1
KV_STEP = 2
HEADS_PER_STEP = 4
VT_ROWS = HEAD_DIM + 16
QK_SCALE = HEAD_DIM ** -0.5 * 1.4426950408889634
VMEM_LIMIT = 56 * 1024 * 1024

SH1, SC1, GT1, SH2, SC2, GT2 = range(6)


def _dot(a, b):
    return jnp.dot(a, b, preferred_element_type=F32)


def _rms(x):
    return x * lax.rsqrt(jnp.mean(x * x, axis=-1, keepdims=True) + EPS)


def _gelu_tanh(x):
    inner = 0.7978845608028654 * (x + 0.044715 * (x * x * x))
    return x * (0.5 * (1.0 + jnp.tanh(inner)))


def _sigmoid(x):
    return 1.0 / (1.0 + jnp.exp(-x))


def _modulated_norm(x, gain, shift, scale):
    return _rms(x) * (gain * (1.0 + scale)) + shift


def _causal_conv3(halo_ref, cols, cur, w):
    rows = cur.shape[0]
    prev = halo_ref[:, cols]
    first = lax.broadcasted_iota(jnp.int32, (CONV_HALO, 1), 0)
    shifted = []
    for lag in (1, 2):
        rolled = pltpu.roll(cur, lag, 0)
        head = jnp.where(first < lag, pltpu.roll(prev, lag, 0), rolled[0:CONV_HALO])
        shifted.append(jnp.concatenate([head, rolled[CONV_HALO:]], axis=0))
    halo_ref[:, cols] = cur[rows - CONV_HALO:rows]
    return w[0:1] * shifted[1] + w[1:2] * shifted[0] + w[2:3] * cur


def _mod_kernel(c_ref, w_ref, b_ref, o_ref):
    c = c_ref[...]
    act = c * _sigmoid(c)
    o_ref[...] = jnp.dot(act, w_ref[...], preferred_element_type=F32,
                         precision=lax.Precision.HIGHEST) + b_ref[...]


def _ada_mod(c, ada_w, ada_b):
    depth, d, n = ada_w.shape
    bsz = c.shape[0]
    rows = 8
    c_pad = jnp.pad(c, ((0, rows - bsz), (0, 0)))
    tn = 1536
    out = pl.pallas_call(
        _mod_kernel,
        grid=(depth, n // tn),
        in_specs=[
            pl.BlockSpec((rows, d), lambda l, j: (0, 0)),
            pl.BlockSpec((None, d, tn), lambda l, j: (l, 0, j)),
            pl.BlockSpec((None, 1, tn), lambda l, j: (l, 0, j)),
        ],
        out_specs=pl.BlockSpec((None, rows, tn), lambda l, j: (l, 0, j)),
        out_shape=jax.ShapeDtypeStruct((depth, rows, n), F32),
        compiler_params=pltpu.CompilerParams(
            dimension_semantics=("arbitrary", "arbitrary"),
            vmem_limit_bytes=VMEM_LIMIT),
        name="ada_mod",
    )(c_pad, ada_w, ada_b.reshape(depth, 1, n))
    return out[:, :bsz].reshape(depth, bsz, 6, d)


def _mixer_ab_kernel(x_ref, mod_ref, g_ref, win_ref, vg_ref, vb_ref, ws_ref, bsf_ref,
                     pw_ref, ps_ref, wout_ref, o_ref, xb_ref, ycat_ref):
    t = pl.program_id(1)
    tm = x_ref.shape[0]

    @pl.when(t == 0)
    def _():
        xb_ref[0:POOL_HALO, :] = jnp.zeros((POOL_HALO, B_WIDTH), F32)

    x = x_ref[...]
    mod = mod_ref[...]
    g = g_ref[...]
    h = _modulated_norm(x, g[0:1], mod[SH1:SH1 + 1], mod[SC1:SC1 + 1]).astype(BF16)

    def gelu_proj(start):
        return jnp.concatenate(
            [_gelu_tanh(_dot(h, win_ref[:, c:c + PROJ_CHUNK]))
             for c in range(start, start + A_WIDTH, PROJ_CHUNK)], axis=1)

    gv = gelu_proj(A_WIDTH)
    xb = _dot(h, win_ref[:, 2 * A_WIDTH:2 * A_WIDTH + B_WIDTH])
    mu = jnp.mean(gv, axis=-1, keepdims=True)
    dv = gv - mu
    var = jnp.mean(dv * dv, axis=-1, keepdims=True)
    vn = (dv * lax.rsqrt(var + EPS) * vg_ref[...] + vb_ref[...]).astype(BF16)
    gu = gelu_proj(0)

    row = lax.broadcasted_iota(jnp.int32, (CHUNK, CHUNK), 0)
    col = lax.broadcasted_iota(jnp.int32, (CHUNK, CHUNK), 1)
    tril = row >= col
    for gi in range(A_GROUPS):
        cols = slice(gi * LANES, (gi + 1) * LANES)
        wg = jnp.where(tril, ws_ref[gi], 0.0).astype(BF16)
        bias = bsf_ref[:, cols]
        for ci in range(tm // CHUNK):
            rows = slice(ci * CHUNK, (ci + 1) * CHUNK)
            mixed = _dot(wg, vn[rows, cols]) + bias
            ycat_ref[rows, cols] = (gu[rows, cols] * mixed).astype(BF16)

    xb_ref[POOL_HALO:POOL_HALO + tm, :] = xb
    pos = t * tm + lax.broadcasted_iota(jnp.int32, (tm, 1), 0)
    for gi, w in enumerate(POOL_WINDOWS):
        cols = slice(gi * LANES, (gi + 1) * LANES)
        s = xb[:, cols]
        for k in range(1, w):
            s = s + xb_ref[POOL_HALO - k:POOL_HALO - k + tm, cols]
        cnt = jnp.minimum(pos + 1, w).astype(F32)
        pooled = s * (1.0 / cnt) - xb[:, cols]
        mixed = _dot(pooled.astype(BF16), pw_ref[gi]) * ps_ref[:, cols]
        ycat_ref[:, A_WIDTH + gi * LANES:A_WIDTH + (gi + 1) * LANES] = mixed.astype(BF16)
    xb_ref[0:POOL_HALO, :] = xb_ref[tm:tm + POOL_HALO, :]

    y = _dot(ycat_ref[...], wout_ref[...])
    o_ref[...] = x + mod[GT1:GT1 + 1] * (_rms(y) * g[1:2])


def _const_spec(shape):
    zeros = (0,) * len(shape)
    return pl.BlockSpec(shape, lambda b, t: zeros, pipeline_mode=pl.Buffered(1))


def _mixer_ab(x, mod, g, w_in, vnorm_g, vnorm_b, w_s, b_s, pool_w, pool_scale, w_out):
    bsz, s, d = x.shape
    tm = TOKEN_TILE
    bias_full = jnp.repeat(b_s.T, LANES, axis=1)
    return pl.pallas_call(
        _mixer_ab_kernel,
        grid=(bsz, s // tm),
        in_specs=[
            pl.BlockSpec((None, tm, d), lambda b, t: (b, t, 0)),
            pl.BlockSpec((None, 6, d), lambda b, t: (b, 0, 0)),
            _const_spec((4, d)),
            _const_spec(w_in.shape),
            _const_spec((1, A_WIDTH)),
            _const_spec((1, A_WIDTH)),
            _const_spec(w_s.shape),
            _const_spec(bias_full.shape),
            _const_spec(pool_w.shape),
            _const_spec((1, B_WIDTH)),
            _const_spec(w_out.shape),
        ],
        out_specs=pl.BlockSpec((None, tm, d), lambda b, t: (b, t, 0)),
        out_shape=jax.ShapeDtypeStruct(x.shape, F32),
        scratch_shapes=[
            pltpu.VMEM((POOL_HALO + tm, B_WIDTH), F32),
            pltpu.VMEM((tm, A_WIDTH + B_WIDTH), BF16),
        ],
        compiler_params=pltpu.CompilerParams(
            dimension_semantics=("arbitrary", "arbitrary"),
            vmem_limit_bytes=VMEM_LIMIT),
        name="mixer_ab",
    )(x, mod, g, w_in.astype(BF16), vnorm_g.reshape(1, -1), vnorm_b.reshape(1, -1),
      w_s, bias_full, pool_w.astype(BF16), pool_scale.reshape(1, -1), w_out.astype(BF16))


def _ffn_kernel(fuse_mixer_out, *refs):
    if fuse_mixer_out:
        (x_ref, yc_ref, yd_ref, wo_ref, mod_ref, g_ref, wup_ref, cw_ref, wdn_ref,
         o_ref, halo_ref, act_ref) = refs
    else:
        (x_ref, mod_ref, g_ref, wup_ref, cw_ref, wdn_ref,
         o_ref, halo_ref, act_ref) = refs
    t = pl.program_id(1)

    @pl.when(t == 0)
    def _():
        halo_ref[...] = jnp.zeros(halo_ref.shape, F32)

    x = x_ref[...]
    mod = mod_ref[...]
    g = g_ref[...]
    if fuse_mixer_out:
        y = _dot(yc_ref[...], wo_ref[0:C_WIDTH, :]) + _dot(yd_ref[...], wo_ref[C_WIDTH:, :])
        x = x + mod[GT1:GT1 + 1] * (_rms(y) * g[1:2])
    h = _modulated_norm(x, g[2:3], mod[SH2:SH2 + 1], mod[SC2:SC2 + 1]).astype(BF16)

    for j in range(D_FF // FF_CHUNK):
        halves = []
        for half in range(2):
            start = half * D_FF + j * FF_CHUNK
            cols = slice(start, start + FF_CHUNK)
            up = _dot(h, wup_ref[:, cols])
            halves.append(_causal_conv3(halo_ref, cols, up, cw_ref[:, cols]))
        gate, lin = halves
        act = gate * _sigmoid(gate) * lin
        act_ref[:, j * FF_CHUNK:(j + 1) * FF_CHUNK] = act.astype(BF16)

    y = _dot(act_ref[...], wdn_ref[...])
    o_ref[...] = x + mod[GT2:GT2 + 1] * (_rms(y) * g[3:4])


def _layer_spec(stacked_shape, layer):
    zeros = (0,) * (len(stacked_shape) - 1)
    return pl.BlockSpec((None,) + tuple(stacked_shape[1:]), lambda b, t: (layer,) + zeros,
                        pipeline_mode=pl.Buffered(1))


def _ffn(x, mod, g, layer, w_up, conv_w, w_down, mixer_out=None):
    bsz, s, d = x.shape
    tm = TOKEN_TILE
    tile = lambda width: pl.BlockSpec((None, tm, width), lambda b, t: (b, t, 0))
    in_specs = [tile(d)]
    args = [x]
    if mixer_out is not None:
        yc, yd, w_o = mixer_out
        in_specs += [tile(C_WIDTH), tile(D_WIDTH), _const_spec(w_o.shape)]
        args += [yc, yd, w_o.astype(BF16)]
    in_specs += [
        pl.BlockSpec((None, 6, d), lambda b, t: (b, 0, 0)),
        _const_spec((4, d)),
        _layer_spec(w_up.shape, layer),
        _layer_spec(conv_w.shape, layer),
        _layer_spec(w_down.shape, layer),
    ]
    args += [mod, g, w_up, conv_w, w_down]
    return pl.pallas_call(
        functools.partial(_ffn_kernel, mixer_out is not None),
        grid=(bsz, s // tm),
        in_specs=in_specs,
        out_specs=tile(d),
        out_shape=jax.ShapeDtypeStruct(x.shape, F32),
        scratch_shapes=[
            pltpu.VMEM((CONV_HALO, 2 * D_FF), F32),
            pltpu.VMEM((tm, D_FF), BF16),
        ],
        compiler_params=pltpu.CompilerParams(
            dimension_semantics=("arbitrary", "arbitrary"),
            vmem_limit_bytes=VMEM_LIMIT),
        name="conv_ffn_fused" if mixer_out is not None else "conv_ffn",
    )(*args)


def _mixer_cd_in_kernel(x_ref, mod_ref, g_ref, pos_ref, invf_ref, win_ref, cw_ref,
                        yc_ref, qt_ref, k_ref, vt_ref, halo_ref):
    t = pl.program_id(1)
    tm = x_ref.shape[0]

    @pl.when(t == 0)
    def _():
        halo_ref[...] = jnp.zeros(halo_ref.shape, F32)

    mod = mod_ref[...]
    g = g_ref[...]
    h = _modulated_norm(x_ref[...], g[0:1], mod[SH1:SH1 + 1], mod[SC1:SC1 + 1]).astype(BF16)

    def proj(i):
        return _dot(h, win_ref[:, i * C_WIDTH:(i + 1) * C_WIDTH])

    prod = proj(1) * proj(2)
    conv = _causal_conv3(halo_ref, slice(0, C_WIDTH), prod, cw_ref[...])
    yc_ref[...] = (proj(0) * conv).astype(BF16)

    half = ROT_DIM // 2
    ang = pos_ref[...].astype(F32) * invf_ref[...]
    lane128 = lax.broadcasted_iota(jnp.int32, (1, LANES), 1)
    lane = lane128 % HEAD_DIM

    def spread(compact):
        rows = jnp.repeat(compact, POS_PER_ROW, axis=0)
        rows = pltpu.roll(rows, 0, 1, stride=ROT_DIM, stride_axis=0)
        base = jnp.where(lane128 < ROT_DIM, rows, 0.0)
        return base + pltpu.roll(base, HEAD_DIM, 1)

    cos = jnp.where(lane < ROT_DIM, spread(jnp.cos(ang)), 1.0)
    sin = spread(jnp.sin(ang))
    sin = jnp.where(lane < half, -sin, jnp.where(lane < ROT_DIM, sin, 0.0))

    def rope(v):
        outs = []
        for ci in range(v.shape[1] // LANES):
            vc = v[:, ci * LANES:(ci + 1) * LANES]
            partner = jnp.where(lane < half, pltpu.roll(vc, LANES - half, 1),
                                pltpu.roll(vc, half, 1))
            outs.append(vc * cos + partner * sin)
        return jnp.concatenate(outs, axis=1)

    q = rope(proj(3)) * QK_SCALE
    k = rope(proj(4))
    v = proj(5)
    ones_rows = jnp.where(
        lax.broadcasted_iota(jnp.int32, (VT_ROWS - HEAD_DIM, MOBA_BLOCK), 0) == 0, 1.0, 0.0)
    for bi in range(tm // MOBA_BLOCK):
        rows = slice(bi * MOBA_BLOCK, (bi + 1) * MOBA_BLOCK)
        qt_ref[bi] = q[rows].T.astype(BF16)
        v_t = v[rows].T
        for hd in range(D_HEADS):
            vt_ref[bi, hd] = jnp.concatenate(
                [v_t[hd * HEAD_DIM:(hd + 1) * HEAD_DIM], ones_rows], axis=0).astype(BF16)
    k_ref[...] = k.astype(BF16)


def _mixer_cd_in(x, mod, g, positions, w_in, conv_w):
    bsz, s, d = x.shape
    tm = TOKEN_TILE
    nb = s // MOBA_BLOCK
    bpt = tm // MOBA_BLOCK
    inv_freq = ROPE_THETA ** (-jnp.arange(0, ROT_DIM, 2, dtype=F32) / ROT_DIM)
    invf = jnp.tile(inv_freq, LANES // inv_freq.shape[0]).reshape(1, LANES)
    slot_pos = (POS_PER_ROW - jnp.arange(POS_PER_ROW)) % POS_PER_ROW
    pos = positions.reshape(bsz, s // POS_PER_ROW, POS_PER_ROW)[:, :, slot_pos]
    pos = jnp.repeat(pos, ROT_DIM, axis=2)
    yc, qt, k, vt = pl.pallas_call(
        _mixer_cd_in_kernel,
        grid=(bsz, s // tm),
        in_specs=[
            pl.BlockSpec((None, tm, d), lambda b, t: (b, t, 0)),
            pl.BlockSpec((None, 6, d), lambda b, t: (b, 0, 0)),
            _const_spec((4, d)),
            pl.BlockSpec((None, tm // POS_PER_ROW, LANES), lambda b, t: (b, t, 0)),
            _const_spec((1, LANES)),
            _const_spec(w_in.shape),
            _const_spec(conv_w.shape),
        ],
        out_specs=[
            pl.BlockSpec((None, tm, C_WIDTH), lambda b, t: (b, t, 0)),
            pl.BlockSpec((None, bpt, D_WIDTH, MOBA_BLOCK), lambda b, t: (b, t, 0, 0)),
            pl.BlockSpec((None, tm, D_WIDTH), lambda b, t: (b, t, 0)),
            pl.BlockSpec((None, bpt, D_HEADS, VT_ROWS, MOBA_BLOCK), lambda b, t: (b, t, 0, 0, 0)),
        ],
        out_shape=[
            jax.ShapeDtypeStruct((bsz, s, C_WIDTH), BF16),
            jax.ShapeDtypeStruct((bsz, nb, D_WIDTH, MOBA_BLOCK), BF16),
            jax.ShapeDtypeStruct((bsz, s, D_WIDTH), BF16),
            jax.ShapeDtypeStruct((bsz, nb, D_HEADS, VT_ROWS, MOBA_BLOCK), BF16),
        ],
        scratch_shapes=[
            pltpu.VMEM((CONV_HALO, C_WIDTH), F32),
        ],
        compiler_params=pltpu.CompilerParams(
            dimension_semantics=("arbitrary", "arbitrary"),
            vmem_limit_bytes=VMEM_LIMIT),
        name="mixer_cd_in",
    )(x, mod, g, pos, invf, w_in.astype(BF16), conv_w)
    return yc, qt, k.reshape(bsz, nb, MOBA_BLOCK, D_WIDTH), vt


def _moba_kernel(qt_ref, k_ref, vt_ref, o_ref, kmean_ref):
    qi = pl.program_id(2)
    nb = k_ref.shape[0]
    width, tq = qt_ref.shape
    heads = width // HEAD_DIM
    neg_inf = -jnp.inf

    @pl.when(qi == 0)
    def _():
        for n in range(nb):
            kmean_ref[n:n + 1, :] = (jnp.sum(k_ref[n].astype(F32), axis=0, keepdims=True)
                                     * (1.0 / MOBA_BLOCK))

    def lanes_of(hd):
        return slice(hd // 2 * LANES, (hd // 2 + 1) * LANES)

    def weights(st, shift):
        return jnp.exp2((st - shift).astype(BF16))

    def tile(n_past):
        groups = [(start, min(KV_GROUP, n_past - start)) for start in range(0, n_past, KV_GROUP)]
        qt = qt_ref[...]
        slab_head = lax.broadcasted_iota(jnp.int32, (LANES, 1), 0) // HEAD_DIM
        lane_head = lax.broadcasted_iota(jnp.int32, (1, width), 1) // HEAD_DIM
        blk = lax.broadcasted_iota(jnp.int32, (nb, tq), 0)
        kpos = lax.broadcasted_iota(jnp.int32, (MOBA_BLOCK, tq), 0)
        qpos = lax.broadcasted_iota(jnp.int32, (MOBA_BLOCK, tq), 1)
        causal = kpos <= qpos
        valid = blk < qi

        qs = [jnp.where(slab_head == hd % 2, qt[lanes_of(hd)], jnp.zeros((LANES, tq), BF16))
              for hd in range(heads)]

        def head_scores(group, hd):
            start, size = group
            return [_dot(k_ref[start + u, :, lanes_of(hd)], qs[hd]) for u in range(size)]

        kmean = kmean_ref[...]
        km_heads = jnp.concatenate(
            [jnp.where(lane_head == hd, kmean, 0.0) for hd in range(heads)], axis=0)
        gates = jnp.dot(km_heads, qt.astype(F32), preferred_element_type=F32,
                        precision=lax.Precision.HIGHEST)
        own = [_dot(k_ref[qi, :, lanes_of(hd)], qs[hd]) for hd in range(heads)]
        nxt = [head_scores(groups[0], hd) for hd in range(heads)] if groups else None

        picks = []
        for hd in range(heads):
            gate = jnp.where(valid, gates[hd * nb:(hd + 1) * nb], neg_inf)
            beats = []
            for m in range(nb):
                gm = gate[m:m + 1, :]
                tie = jnp.where(blk > m, 1, 0)
                beats.append(jnp.where(gm > gate, 1, jnp.where(gm == gate, tie, 0)))
            while len(beats) > 1:
                beats = [a + b for a, b in zip(beats[0::2], beats[1::2])]
            picks.append(jnp.where(valid, jnp.where(beats[0] < MOBA_TOPK, 1.0, 0.0), 0.0))

        state = []
        for hd in range(heads):
            st = jnp.where(causal, own[hd], neg_inf)
            m0 = jnp.max(st, axis=0, keepdims=True)
            state.append((m0, _dot(vt_ref[qi, hd], weights(st, m0))))

        for gi, (start, size) in enumerate(groups):
            cur = nxt
            nxt = []
            for hd in range(heads):
                if gi + 1 < len(groups):
                    nxt.append(head_scores(groups[gi + 1], hd))
                m_run, acc = state[hd]
                picked = []
                m_new = m_run
                for u in range(size):
                    pick = picks[hd][start + u:start + u + 1, :] > 0.5
                    col_max = jnp.max(cur[hd][u], axis=0, keepdims=True)
                    m_new = jnp.maximum(m_new, jnp.where(pick, col_max, neg_inf))
                    picked.append(pick)
                acc = jnp.exp2(m_run - m_new) * acc
                for u in range(size):
                    p = weights(cur[hd][u], jnp.where(picked[u], m_new, jnp.inf))
                    acc = acc + _dot(vt_ref[start + u, hd], p)
                state[hd] = (m_new, acc)

        out_t = jnp.concatenate(
            [acc[0:HEAD_DIM, :] * (1.0 / acc[HEAD_DIM:HEAD_DIM + 1, :]) for _, acc in state],
            axis=0)
        o_ref[...] = out_t.T.astype(BF16)

    steps = (qi + KV_STEP - 1) // KV_STEP
    for count in range(nb // KV_STEP + 1):
        pl.when(steps == count)(functools.partial(tile, count * KV_STEP))


def _moba_attention(qt, k, vt):
    bsz, nb, dw, blk = qt.shape
    heads = HEADS_PER_STEP
    width = heads * HEAD_DIM
    s = nb * blk
    return pl.pallas_call(
        _moba_kernel,
        grid=(bsz, dw // width, nb),
        in_specs=[
            pl.BlockSpec((None, None, width, blk), lambda b, p, i: (b, i, p, 0)),
            pl.BlockSpec((None, nb, blk, width), lambda b, p, i: (b, 0, 0, p)),
            pl.BlockSpec((None, nb, heads, VT_ROWS, blk), lambda b, p, i: (b, 0, p, 0, 0)),
        ],
        out_specs=pl.BlockSpec((None, blk, width), lambda b, p, i: (b, i, p)),
        out_shape=jax.ShapeDtypeStruct((bsz, s, dw), BF16),
        scratch_shapes=[pltpu.VMEM((nb, width), F32)],
        compiler_params=pltpu.CompilerParams(
            dimension_semantics=("arbitrary", "arbitrary", "arbitrary"),
            vmem_limit_bytes=VMEM_LIMIT),
        name="moba_attention",
    )(qt, k, vt)


def kernel(x, c, positions, ab_w_in, ab_vnorm_g, ab_vnorm_b, ab_spatial_w, ab_spatial_b,
           ab_pool_w, ab_pool_scale, ab_w_out, cd_w_in, cd_conv_w, cd_w_out,
           ffn_w_up, ffn_conv_w, ffn_w_down, ada_w, ada_b, norm_g):
    depth = ada_w.shape[0]
    mod = _ada_mod(c, ada_w, ada_b)
    ffn = (ffn_w_up.astype(BF16), ffn_conv_w, ffn_w_down.astype(BF16))
    for i in range(depth):
        j = i // 2
        if i % 2 == 0:
            x = _mixer_ab(x, mod[i], norm_g[i], ab_w_in[j], ab_vnorm_g[j], ab_vnorm_b[j],
                          ab_spatial_w[j], ab_spatial_b[j], ab_pool_w[j], ab_pool_scale[j],
                          ab_w_out[j])
            x = _ffn(x, mod[i], norm_g[i], i, *ffn)
        else:
            yc, qt, k, vt = _mixer_cd_in(x, mod[i], norm_g[i], positions, cd_w_in[j], cd_conv_w[j])
            yd = _moba_attention(qt, k, vt)
            x = _ffn(x, mod[i], norm_g[i], i, *ffn, mixer_out=(yc, yd, cd_w_out[j]))
    return x
```
